```python
import math
import jax
import jax.numpy as jnp
from jax import lax
import numpy as np

D_MODEL = 2048
BATCH = 4
SEQ = 2048
DEPTH = 2
DEC_BATCH = 8
DEC_SEQ = 4
PAST_LEN = 16384
PAGE_SIZE = 128

N_A_LAYERS = DEPTH // 2
N_B_LAYERS = DEPTH - N_A_LAYERS
HG_HEAD_K = 128
HG_HEADS = D_MODEL // HG_HEAD_K
HG_HEAD_V = D_MODEL // HG_HEADS
HG_CHUNK = 64
SB_HEAD_DIM = 128
SB_HEADS = D_MODEL // SB_HEAD_DIM
SB_QUERY_BLOCK = 128
SB_BIAS_INIT = -9.0
N_EXPERTS = 64
TOP_K = 8
D_EXPERT = D_MODEL // 4
D_SHARED = D_EXPERT
ROUTED_SCALE = 2.5
MOE_BLOCK = 128
NORM_EPS = 1e-6

kernel_name = 'yoco_hgrn2_stickbreaking_moe_step'


def rms_norm(x, g):
    xf = x.astype(jnp.float32)
    y = xf * lax.rsqrt(jnp.mean(xf * xf, axis=-1, keepdims=True) + NORM_EPS)
    return (y * g.astype(jnp.float32)).astype(x.dtype)


def ada_modulate(x, g, shift, scale):
    return rms_norm(x, g) * (1 + scale[:, None, :]) + shift[:, None, :]


def swiglu(x, wg, wu, wd):
    return (jax.nn.silu(x @ wg) * (x @ wu)) @ wd


def hgrn2_recurrence(q, log_f, k, v, s0):
    B, T, H, K = q.shape
    V = v.shape[-1]
    C = math.gcd(T, HG_CHUNK)
    n = T // C

    def chunks(a):
        return a.reshape(B, n, C, H, a.shape[-1]).transpose(1, 0, 3, 2, 4)

    causal = jnp.tril(jnp.ones((C, C), dtype=bool))

    def step(S, inp):
        qc, lc, kc, vc = inp
        b = jnp.cumsum(lc, axis=2)
        o_inter = jnp.einsum('bhtk,bhkv->bhtv', qc * jnp.exp(b), S)
        decay = jnp.exp(jnp.where(causal[:, :, None], b[:, :, :, None, :] - b[:, :, None, :, :], -jnp.inf))
        scores = jnp.einsum('bhtk,bhtsk,bhsk->bhts', qc, decay, kc)
        o_intra = jnp.einsum('bhts,bhsv->bhtv', scores, vc)
        b_end = b[:, :, -1:, :]
        S_new = jnp.exp(b_end[:, :, 0, :])[..., None] * S + jnp.einsum('bhsk,bhsv->bhkv', kc * jnp.exp(b_end - b), vc)
        return S_new, o_inter + o_intra

    S, o = lax.scan(step, s0, (chunks(q), chunks(log_f), chunks(k), chunks(v)))
    o = o.transpose(1, 0, 3, 2, 4).reshape(B, T, H, V)
    return o, S


def hgrn2_mixer(h, w_in, lb, norm_g, w_out, s0):
    B, T, _ = h.shape
    HK = HG_HEADS * HG_HEAD_K
    HV = HG_HEADS * HG_HEAD_V
    q, fu, i, g = jnp.split(h @ w_in, [HK, 2 * HK, 2 * HK + HV], axis=-1)
    f = lb + (1 - lb) * jax.nn.sigmoid(fu.astype(jnp.float32))
    log_f = jnp.log(f)
    kk = 1 - f
    heads = lambda a, d: a.reshape(B, T, HG_HEADS, d)
    o, S = hgrn2_recurrence(heads(q.astype(jnp.float32), HG_HEAD_K), heads(log_f, HG_HEAD_K),
                            heads(kk, HG_HEAD_K), heads(i.astype(jnp.float32), HG_HEAD_V),
                            s0.astype(jnp.float32))
    o = rms_norm(o, norm_g.reshape(HG_HEADS, HG_HEAD_V)).reshape(B, T, HV).astype(h.dtype)
    o = o * jax.nn.silu(g)
    return o @ w_out, S.astype(s0.dtype)


def stick_breaking_attention(q, k, v, bias, q_offset):
    B, Tq, H, d = q.shape
    scale = d ** -0.5
    bias_f = bias.astype(jnp.float32)[None, :, None, None]
    outs = []
    for start in range(0, Tq, SB_QUERY_BLOCK):
        end = min(start + SB_QUERY_BLOCK, Tq)
        n_keys = q_offset + end
        qb = q[:, start:end]
        kb = k[:, :n_keys]
        vb = v[:, :n_keys]
        z = jnp.einsum('bqhd,bkhd->bhqk', qb, kb).astype(jnp.float32) * scale + bias_f
        t_pos = q_offset + start + jnp.arange(end - start)[:, None]
        s_pos = jnp.arange(n_keys)[None, :]
        visible = s_pos < t_pos
        log_keep = jnp.where(visible, jax.nn.log_sigmoid(-z), 0.0)
        log_after = lax.cumsum(log_keep, axis=3, reverse=True) - log_keep
        A = jnp.where(visible, jnp.exp(jax.nn.log_sigmoid(z) + log_after), 0.0)
        outs.append(jnp.einsum('bhqk,bkhd->bqhd', A.astype(v.dtype), vb))
    return jnp.concatenate(outs, axis=1)


def stick_breaking_mixer(h, k_all, v_all, q_offset, w_q, q_gain, logit_bias, w_out):
    B, T, _ = h.shape
    q = rms_norm((h @ w_q).reshape(B, T, SB_HEADS, SB_HEAD_DIM), q_gain)
    o = stick_breaking_attention(q, k_all, v_all, logit_bias, q_offset)
    return o.reshape(B, T, SB_HEADS * SB_HEAD_DIM) @ w_out


def shared_kv(x, c_act, kv_ada_w, kv_ada_b, kv_norm, w_kv, k_norm):
    B, T, _ = x.shape
    shift, scale = jnp.split(c_act @ kv_ada_w + kv_ada_b, 2, axis=-1)
    h = ada_modulate(x, kv_norm, shift, scale)
    k, v = jnp.split(h @ w_kv, 2, axis=-1)
    k = rms_norm(k.reshape(B, T, SB_HEADS, SB_HEAD_DIM), k_norm)
    v = v.reshape(B, T, SB_HEADS, SB_HEAD_DIM)
    return k, v


def grouped_experts(x, idx, w_gate, w_up, w_down):
    N, D = x.shape
    NK = N * TOP_K
    blk = max(8, min(MOE_BLOCK, NK // N_EXPERTS))
    n_blocks = -(-(NK + N_EXPERTS * (blk - 1)) // blk)
    flat_e = idx.reshape(-1).astype(jnp.int32)
    flat_tok = jnp.arange(NK, dtype=jnp.int32) // TOP_K
    order = jnp.argsort(flat_e)
    sorted_e = flat_e[order]
    counts = jnp.bincount(flat_e, length=N_EXPERTS)
    padded = (counts + blk - 1) // blk * blk
    start = jnp.cumsum(counts) - counts
    pend = jnp.cumsum(padded)
    pstart = pend - padded
    dest = pstart[sorted_e] + jnp.arange(NK, dtype=jnp.int32) - start[sorted_e]
    rows = jnp.full((n_blocks * blk,), N, dtype=jnp.int32).at[dest].set(flat_tok[order])
    block_e = jnp.minimum(jnp.searchsorted(pend, jnp.arange(n_blocks, dtype=jnp.int32) * blk, side='right'), N_EXPERTS - 1)
    xp = jnp.concatenate([x, jnp.zeros((1, D), x.dtype)], axis=0)
    xb = xp[rows].reshape(n_blocks, blk, D)

    def run(args):
        xe, e = args
        return swiglu(xe, w_gate[e], w_up[e], w_down[e])

    yb = lax.map(run, (xb, block_e)).reshape(n_blocks * blk, D)
    y = jnp.zeros((NK, D), yb.dtype).at[order].set(yb[dest])
    return y.reshape(N, TOP_K, D)


def moe_ffn(h, router_w, router_bias, w_gate, w_up, w_down, ws_gate, ws_up, ws_down):
    B, T, D = h.shape
    x = h.reshape(B * T, D)
    scores = jax.nn.sigmoid((x @ router_w).astype(jnp.float32))
    _, idx = lax.top_k(scores + router_bias.astype(jnp.float32), TOP_K)
    w = jnp.take_along_axis(scores, idx, axis=-1)
    w = w / jnp.sum(w, axis=-1, keepdims=True) * ROUTED_SCALE
    routed = grouped_experts(x, idx, w_gate, w_up, w_down)
    y = jnp.einsum('nkd,nk->nd', routed, w.astype(routed.dtype)) + swiglu(x, ws_gate, ws_up, ws_down)
    return y.reshape(B, T, D)


def trunk(x, c, s0, past_k, past_v, ada_w, ada_b, norm_mix, norm_ffn, hg_w_in, hg_lb, hg_norm, hg_w_out,
          kv_ada_w, kv_ada_b, kv_norm, w_kv, k_norm, sb_w_q, q_norm, sb_logit_bias, sb_w_out,
          router_w, router_bias, exp_w_gate, exp_w_up, exp_w_down, sh_w_gate, sh_w_up, sh_w_down):
    c_act = jax.nn.silu(c)
    lower_bounds = jnp.cumsum(jax.nn.softmax(hg_lb.astype(jnp.float32), axis=0), axis=0)
    past_len = 0 if past_k is None else past_k.shape[1]
    new_states = []
    k_new = v_new = k_all = v_all = None
    for l in range(DEPTH):
        sh_m, sc_m, g_m, sh_f, sc_f, g_f = jnp.split(c_act @ ada_w[l] + ada_b[l], 6, axis=-1)
        if l == N_A_LAYERS:
            k_new, v_new = shared_kv(x, c_act, kv_ada_w, kv_ada_b, kv_norm, w_kv, k_norm)
            if past_k is None:
                k_all, v_all = k_new, v_new
            else:
                k_all = jnp.concatenate([past_k.astype(k_new.dtype), k_new], axis=1)
                v_all = jnp.concatenate([past_v.astype(v_new.dtype), v_new], axis=1)
        h = ada_modulate(x, norm_mix[l], sh_m, sc_m)
        if l < N_A_LAYERS:
            y, s = hgrn2_mixer(h, hg_w_in[l], lower_bounds[l], hg_norm[l], hg_w_out[l], s0[l])
            new_states.append(s)
        else:
            j = l - N_A_LAYERS
            y = stick_breaking_mixer(h, k_all, v_all, past_len, sb_w_q[j], q_norm[j], sb_logit_bias[j], sb_w_out[j])
        x = x + g_m[:, None, :] * y
        h = ada_modulate(x, norm_ffn[l], sh_f, sc_f)
        x = x + g_f[:, None, :] * moe_ffn(h, router_w[l], router_bias[l], exp_w_gate[l], exp_w_up[l], exp_w_down[l],
                                          sh_w_gate[l], sh_w_up[l], sh_w_down[l])
    return x, jnp.stack(new_states), k_new, v_new


def setup_inputs(seed: int = 0):
    key = jax.random.key(seed)
    ks = jax.random.split(key, 40)
    f32 = jnp.float32
    D = D_MODEL
    HK = HG_HEADS * HG_HEAD_K
    HV = HG_HEADS * HG_HEAD_V
    HS = SB_HEADS * SB_HEAD_DIM
    n_pages = PAST_LEN // PAGE_SIZE
    used = DEC_BATCH * n_pages
    n_phys = used + max(1, used // 4)

    def nrm(k, shape, scale):
        return jax.random.normal(k, shape, f32) * scale

    def gain(k, shape):
        return 1.0 + 0.02 * jax.random.normal(k, shape, f32)

    page_table = jax.random.permutation(ks[7], n_phys)[:used].reshape(DEC_BATCH, n_pages).astype(jnp.int32)
    return {
        'x_prompt': nrm(ks[0], (BATCH, SEQ, D), 1.0),
        'x_sample': nrm(ks[1], (DEC_BATCH, DEC_SEQ, D), 1.0),
        'c_prompt': nrm(ks[2], (BATCH, D), 1.0),
        'c_sample': nrm(ks[3], (DEC_BATCH, D), 1.0),
        'state_hgrn': nrm(ks[4], (N_A_LAYERS, DEC_BATCH, HG_HEADS, HG_HEAD_K, HG_HEAD_V), 0.5),
        'cache_k': nrm(ks[5], (n_phys, PAGE_SIZE, SB_HEADS, SB_HEAD_DIM), 1.0),
        'cache_v': nrm(ks[6], (n_phys, PAGE_SIZE, SB_HEADS, SB_HEAD_DIM), 1.0),
        'page_table': page_table,
        'ada_w': nrm(ks[8], (DEPTH, D, 6 * D), 0.5 * D ** -0.5),
        'ada_b': nrm(ks[9], (DEPTH, 6 * D), 0.02),
        'norm_mix': gain(ks[10], (DEPTH, D)),
        'norm_ffn': gain(ks[11], (DEPTH, D)),
        'hg_w_in': nrm(ks[12], (N_A_LAYERS, D, 2 * HK + HV + D), D ** -0.5),
        'hg_lb': nrm(ks[13], (N_A_LAYERS + 1, HK), 1.0),
        'hg_norm': gain(ks[14], (N_A_LAYERS, HV)),
        'hg_w_out': nrm(ks[15], (N_A_LAYERS, HV, D), HV ** -0.5),
        'kv_ada_w': nrm(ks[16], (D, 2 * D), 0.5 * D ** -0.5),
        'kv_ada_b': nrm(ks[17], (2 * D,), 0.02),
        'kv_norm': gain(ks[18], (D,)),
        'w_kv': nrm(ks[19], (D, 2 * HS), D ** -0.5),
        'k_norm': gain(ks[20], (SB_HEAD_DIM,)),
        'sb_w_q': nrm(ks[21], (N_B_LAYERS, D, HS), D ** -0.5),
        'q_norm': gain(ks[22], (N_B_LAYERS, SB_HEAD_DIM)),
        'sb_logit_bias': SB_BIAS_INIT + nrm(ks[32], (N_B_LAYERS, SB_HEADS), 0.5),
        'sb_w_out': nrm(ks[23], (N_B_LAYERS, HS, D), HS ** -0.5),
        'router_w': nrm(ks[24], (DEPTH, D, N_EXPERTS), D ** -0.5),
        'router_bias': nrm(ks[25], (DEPTH, N_EXPERTS), 0.01),
        'exp_w_gate': nrm(ks[26], (DEPTH, N_EXPERTS, D, D_EXPERT), D ** -0.5),
        'exp_w_up': nrm(ks[27], (DEPTH, N_EXPERTS, D, D_EXPERT), D ** -0.5),
        'exp_w_down': nrm(ks[28], (DEPTH, N_EXPERTS, D_EXPERT, D), D_EXPERT ** -0.5),
        'sh_w_gate': nrm(ks[29], (DEPTH, D, D_SHARED), D ** -0.5),
        'sh_w_up': nrm(ks[30], (DEPTH, D, D_SHARED), D ** -0.5),
        'sh_w_down': nrm(ks[31], (DEPTH, D_SHARED, D), D_SHARED ** -0.5),
    }


def reference(x_prompt, x_sample, c_prompt, c_sample, state_hgrn, cache_k, cache_v, page_table,
              ada_w, ada_b, norm_mix, norm_ffn, hg_w_in, hg_lb, hg_norm, hg_w_out,
              kv_ada_w, kv_ada_b, kv_norm, w_kv, k_norm, sb_w_q, q_norm, sb_logit_bias, sb_w_out,
              router_w, router_bias, exp_w_gate, exp_w_up, exp_w_down, sh_w_gate, sh_w_up, sh_w_down):
    weights = (ada_w, ada_b, norm_mix, norm_ffn, hg_w_in, hg_lb, hg_norm, hg_w_out,
               kv_ada_w, kv_ada_b, kv_norm, w_kv, k_norm, sb_w_q, q_norm, sb_logit_bias, sb_w_out,
               router_w, router_bias, exp_w_gate, exp_w_up, exp_w_down, sh_w_gate, sh_w_up, sh_w_down)
    s0_prompt = jnp.zeros((N_A_LAYERS, x_prompt.shape[0], HG_HEADS, HG_HEAD_K, HG_HEAD_V), state_hgrn.dtype)
    y_prompt, s_prompt, k_prompt, v_prompt = trunk(x_prompt, c_prompt, s0_prompt, None, None, *weights)
    n_dec = x_sample.shape[0]
    past_k = cache_k[page_table].reshape(n_dec, -1, SB_HEADS, SB_HEAD_DIM)
    past_v = cache_v[page_table].reshape(n_dec, -1, SB_HEADS, SB_HEAD_DIM)
    y_sample, s_sample, k_sample, v_sample = trunk(x_sample, c_sample, state_hgrn, past_k, past_v, *weights)
    return (y_prompt, y_sample, s_prompt, s_sample, k_prompt, v_prompt, k_sample, v_sample)
```

```python
import functools
import math

import numpy as np
import jax
import jax.numpy as jnp
from jax import lax
from jax.experimental import pallas as pl
from jax.experimental.pallas import tpu as pltpu

F32 = jnp.float32
BF16 = jnp.bfloat16

LANES = 128
VMEM_LIMIT_BYTES = 52 * 1024 * 1024
NORM_EPS = 1e-6
HEAD_DIM = 128
N_EXPERTS = 64
TOP_K = 8
ROUTED_SCALE = 2.5
HG_CHUNK = 64


def _cparams(sem):
    return pltpu.CompilerParams(dimension_semantics=sem, vmem_limit_bytes=VMEM_LIMIT_BYTES)


def _sigmoid(x):
    return 1.0 / (1.0 + jnp.exp(-x))


def _dot(a, b):
    return jnp.dot(a, b, preferred_element_type=F32)


def _dot_nt(a, b):
    return lax.dot_general(a, b, (((1,), (1,)), ((), ())), preferred_element_type=F32)


def _dot_tn(a, b):
    return lax.dot_general(a, b, (((0,), (0,)), ((), ())), preferred_element_type=F32)


def _norm_mod_kernel(x_ref, g_ref, sh_ref, sc_ref, *rest, with_router):
    x = x_ref[...]
    ms = jnp.mean(x * x, axis=-1, keepdims=True)
    y = x * lax.rsqrt(ms + NORM_EPS) * g_ref[...]
    h = y * (1.0 + sc_ref[...]) + sh_ref[...]
    if with_router:
        rw_ref, o_ref, logit_ref = rest
        logit_ref[...] = jnp.dot(h, rw_ref[...], preferred_element_type=F32, precision=lax.Precision.HIGHEST)
    else:
        o_ref, = rest
    o_ref[...] = h.astype(o_ref.dtype)


def norm_mod(x, gain, shift, scale, rows_per_group, tm, router_w=None, router_layer=None):
    N, D = x.shape
    tm = min(tm, N)
    with_router = router_w is not None
    if rows_per_group % tm == 0:
        bpg = rows_per_group // tm
        sh = shift.reshape(-1, 1, D)
        sc = scale.reshape(-1, 1, D)
        mod_spec = pl.BlockSpec((None, 1, D), lambda i: (i // bpg, 0, 0))
    else:
        sh = jnp.repeat(shift, rows_per_group, axis=0)
        sc = jnp.repeat(scale, rows_per_group, axis=0)
        mod_spec = pl.BlockSpec((tm, D), lambda i: (i, 0))
    args = [x, gain.reshape(1, D), sh, sc]
    in_specs = [pl.BlockSpec((tm, D), lambda i: (i, 0)),
                pl.BlockSpec((1, D), lambda i: (0, 0)),
                mod_spec, mod_spec]
    out_specs = pl.BlockSpec((tm, D), lambda i: (i, 0))
    out_shape = jax.ShapeDtypeStruct((N, D), BF16)
    if with_router:
        E = router_w.shape[-1]
        args.append(router_w)
        in_specs.append(pl.BlockSpec((None, D, E), lambda i: (router_layer, 0, 0)))
        out_specs = [out_specs, pl.BlockSpec((tm, E), lambda i: (i, 0))]
        out_shape = [out_shape, jax.ShapeDtypeStruct((N, E), F32)]
    return pl.pallas_call(
        functools.partial(_norm_mod_kernel, with_router=with_router),
        grid=(N // tm,),
        in_specs=in_specs,
        out_specs=out_specs,
        out_shape=out_shape,
        compiler_params=_cparams(("arbitrary",)),
        name="norm_router" if with_router else "norm_mod",
    )(*args)


def _mm_kernel(*refs, silu_in, has_bias, epilogue, tn):
    it = iter(refs)
    x_ref = next(it)
    w_ref = next(it)
    b_ref = next(it) if has_bias else None
    if epilogue == "headnorm":
        hg_ref = next(it)
    elif epilogue == "resid":
        r_ref = next(it)
        gate_ref = next(it)
    o_ref = next(it)
    wbf_ref = next(it)

    @pl.when(pl.program_id(1) == 0)
    def _():
        wbf_ref[...] = w_ref[...].astype(BF16)

    x = x_ref[...]
    if silu_in:
        x = x.astype(F32)
        x = x * _sigmoid(x)
    y = _dot(x.astype(BF16), wbf_ref[...])
    if has_bias:
        y = y + b_ref[...]
    if epilogue == "headnorm":
        gain = hg_ref[...]
        for c in range(tn // HEAD_DIM):
            sl = slice(c * HEAD_DIM, (c + 1) * HEAD_DIM)
            yc = y[:, sl]
            ms = jnp.mean(yc * yc, axis=-1, keepdims=True)
            o_ref[:, sl] = (yc * lax.rsqrt(ms + NORM_EPS) * gain).astype(o_ref.dtype)
    elif epilogue == "resid":
        o_ref[...] = (r_ref[...] + gate_ref[...] * y).astype(o_ref.dtype)
    else:
        o_ref[...] = y.astype(o_ref.dtype)


def mm(x, w, *, w_layer=None, w_col0=0, n_out=None, bias=None, silu_in=False, out_dtype=F32,
       tm=512, tn=512, epilogue=None, head_gain=None, resid=None, gate=None, rows_per_group=None,
       name="mm"):
    M, K = x.shape
    Ntot = w.shape[-1]
    N = Ntot - w_col0 if n_out is None else n_out
    tm = min(tm, M)
    tn = min(tn, N)
    assert M % tm == 0 and N % tn == 0 and w_col0 % tn == 0
    c0 = w_col0 // tn
    if w.ndim == 3:
        w_spec = pl.BlockSpec((None, K, tn), lambda j, i: (w_layer, 0, j + c0))
    else:
        w_spec = pl.BlockSpec((K, tn), lambda j, i: (0, j + c0))
    args = [x, w]
    in_specs = [pl.BlockSpec((tm, K), lambda j, i: (i, 0)), w_spec]
    if bias is not None:
        args.append(bias.reshape(1, -1))
        in_specs.append(pl.BlockSpec((1, tn), lambda j, i: (0, j + c0)))
    if epilogue == "headnorm":
        args.append(head_gain.reshape(1, HEAD_DIM))
        in_specs.append(pl.BlockSpec((1, HEAD_DIM), lambda j, i: (0, 0)))
    elif epilogue == "resid":
        args.append(resid)
        in_specs.append(pl.BlockSpec((tm, tn), lambda j, i: (i, j)))
        if rows_per_group % tm == 0:
            bpg = rows_per_group // tm
            args.append(gate.reshape(-1, 1, N))
            in_specs.append(pl.BlockSpec((None, 1, tn), lambda j, i: (i // bpg, 0, j)))
        else:
            args.append(jnp.repeat(gate, rows_per_group, axis=0))
            in_specs.append(pl.BlockSpec((tm, tn), lambda j, i: (i, j)))
    kern = functools.partial(_mm_kernel, silu_in=silu_in, has_bias=bias is not None,
                             epilogue=epilogue, tn=tn)
    return pl.pallas_call(
        kern,
        grid=(N // tn, M // tm),
        in_specs=in_specs,
        out_specs=pl.BlockSpec((tm, tn), lambda j, i: (i, j)),
        out_shape=jax.ShapeDtypeStruct((M, N), out_dtype),
        scratch_shapes=[pltpu.VMEM((K, tn), BF16)],
        compiler_params=_cparams(("arbitrary", "arbitrary")),
        name=name,
    )(*args)


def _hgrn_constants(C):
    r = np.arange(C)
    mats = [(r[None, :] <= r[:, None]).astype(np.float32)]
    masks = [np.eye(C, dtype=np.float32)]
    m = C // 2
    while m >= 1:
        base = (r // (2 * m)) * (2 * m)
        pivot = base + m - 1
        upper = r > pivot
        j = r[None, :]
        mat = np.where(upper[:, None], (j > pivot[:, None]) & (j <= r[:, None]),
                       (j > r[:, None]) & (j <= pivot[:, None]))
        mats.append(mat.astype(np.float32))
        same = base[:, None] == base[None, :]
        masks.append((same & upper[:, None] & (~upper)[None, :]).astype(np.float32))
        m //= 2
    return np.concatenate(mats, axis=0), np.stack(masks, axis=0)


def _hgrn_kernel(q_ref, fu_ref, i_ref, g_ref, lb_ref, gain_ref, s0_ref, mat_ref, mask_ref,
                 o_ref, sout_ref, s_ref, *, C, n_chunks, n_levels, t_valid):
    t = pl.program_id(2)

    @pl.when(t == 0)
    def _():
        s_ref[...] = s0_ref[...].T

    lb = lb_ref[...]
    gain = gain_ref[...]
    for c in range(n_chunks):
        rows = slice(c * C, (c + 1) * C)
        q = q_ref[rows, :]
        f = lb + (1.0 - lb) * _sigmoid(fu_ref[rows, :])
        logf = jnp.log(f)
        kk = 1.0 - f
        if t_valid is not None:
            valid = (lax.broadcasted_iota(jnp.int32, (C, HEAD_DIM), 0) + c * C) < t_valid
            logf = jnp.where(valid, logf, 0.0)
            kk = jnp.where(valid, kk, 0.0)
        v = i_ref[rows, :]
        e = jnp.dot(mat_ref[...], logf, preferred_element_type=F32,
                    precision=lax.Precision.HIGHEST)
        b = e[0:C, :]
        scores = mask_ref[0] * _dot_nt(q.astype(BF16), kk.astype(BF16))
        for l in range(1, n_levels + 1):
            x = jnp.exp(e[l * C:(l + 1) * C, :])
            scores = scores + mask_ref[l] * _dot_nt((q * x).astype(BF16), (kk * x).astype(BF16))
        St = s_ref[...]
        vb = v.astype(BF16)
        o = _dot_nt((q * jnp.exp(b)).astype(BF16), St.astype(BF16)) + _dot(scores.astype(BF16), vb)
        b_end = b[C - 1:C, :]
        khat = (kk * jnp.exp(b_end - b)).astype(BF16)
        s_ref[...] = jnp.exp(b_end) * St + _dot_tn(vb, khat)
        ms = jnp.mean(o * o, axis=-1, keepdims=True)
        gt = g_ref[rows, :]
        o_ref[rows, :] = (o * lax.rsqrt(ms + NORM_EPS) * gain * (gt * _sigmoid(gt))).astype(o_ref.dtype)

    @pl.when(t == pl.num_programs(2) - 1)
    def _():
        sout_ref[...] = s_ref[...].T


def hgrn2(qfig, lb, gain, s0, *, B, T, C, tb, t_valid=None):
    H = s0.shape[1]
    HD = H * HEAD_DIM
    x3 = qfig.reshape(B, T, 4 * HD)
    mats, masks = _hgrn_constants(C)
    n_levels = masks.shape[0] - 1
    kern = functools.partial(_hgrn_kernel, C=C, n_chunks=tb // C, n_levels=n_levels, t_valid=t_valid)

    def col(off):
        return pl.BlockSpec((None, tb, HEAD_DIM), lambda b, h, t: (b, t, h + off * H))

    head_vec = pl.BlockSpec((1, HEAD_DIM), lambda b, h, t: (0, h))
    o, s_out = pl.pallas_call(
        kern,
        grid=(B, H, T // tb),
        in_specs=[col(0), col(1), col(2), col(3), head_vec, head_vec,
                  pl.BlockSpec((None, None, HEAD_DIM, HEAD_DIM), lambda b, h, t: (b, h, 0, 0)),
                  pl.BlockSpec(mats.shape, lambda b, h, t: (0, 0)),
                  pl.BlockSpec(masks.shape, lambda b, h, t: (0, 0, 0))],
        out_specs=[pl.BlockSpec((None, tb, HEAD_DIM), lambda b, h, t: (b, t, h)),
                   pl.BlockSpec((None, None, HEAD_DIM, HEAD_DIM), lambda b, h, t: (b, h, 0, 0))],
        out_shape=[jax.ShapeDtypeStruct((B, T, HD), BF16),
                   jax.ShapeDtypeStruct(s0.shape, F32)],
        scratch_shapes=[pltpu.VMEM((HEAD_DIM, HEAD_DIM), F32)],
        compiler_params=_cparams(("arbitrary", "arbitrary", "arbitrary")),
        name="hgrn2",
    )(x3, x3, x3, x3, lb.reshape(1, HD), gain.reshape(1, HD), s0,
      jnp.asarray(mats), jnp.asarray(masks))
    return o.reshape(B * T, HD), s_out


def _log_sig_pair(z):
    tail = jnp.log1p(jnp.exp(-jnp.abs(z)))
    return jnp.minimum(z, 0.0) - tail, -jnp.maximum(z, 0.0) - tail


def _suffix_sum(lk, u):
    hi = lk.astype(BF16)
    lo = (lk - hi.astype(F32)).astype(BF16)
    return _dot(hi, u) + _dot(lo, u)


def _sb_block(z, visible, u, rsum):
    ls, lk = _log_sig_pair(z)
    if visible is not None:
        lk = jnp.where(visible, lk, 0.0)
    la = _suffix_sum(lk, u) + rsum
    a = jnp.exp(ls + la)
    if visible is not None:
        a = jnp.where(visible, a, 0.0)
    return a, rsum + jnp.sum(lk, axis=-1, keepdims=True)


def _sb_prompt_kernel(bias_ref, q_ref, k_ref, v_ref, u_ref, o_ref, *, tq, scale):
    h = pl.program_id(1)
    qi = pl.program_id(2)
    bias = bias_ref[h]
    q = q_ref[...].astype(BF16)
    u = u_ref[...]

    def block(kb, visible, rsum, acc):
        start = pl.multiple_of(kb * tq, tq)
        k = k_ref[pl.ds(start, tq), :].astype(BF16)
        v = v_ref[pl.ds(start, tq), :].astype(BF16)
        z = _dot_nt(q, k) * scale + bias
        a, rsum = _sb_block(z, visible, u, rsum)
        return rsum, acc + _dot(a.astype(BF16), v)

    tri = lax.broadcasted_iota(jnp.int32, (tq, tq), 1) < lax.broadcasted_iota(jnp.int32, (tq, tq), 0)
    rsum, acc = block(qi, tri, jnp.zeros((tq, 1), F32), jnp.zeros((tq, HEAD_DIM), F32))

    def body(it, carry):
        return block(qi - 1 - it, None, *carry)

    rsum, acc = lax.fori_loop(0, qi, body, (rsum, acc))
    o_ref[...] = acc.astype(o_ref.dtype)


def sb_attention_prompt(q, k, v, bias, *, B, T, tq=256):
    N, HD = q.shape
    H = HD // HEAD_DIM
    nq = T // tq
    u = jnp.asarray(np.triu(np.ones((tq, tq), np.float32), 0).T - np.eye(tq, dtype=np.float32), BF16)
    kern = functools.partial(_sb_prompt_kernel, tq=tq, scale=HEAD_DIM ** -0.5)
    return pl.pallas_call(
        kern,
        grid_spec=pltpu.PrefetchScalarGridSpec(
            num_scalar_prefetch=1,
            grid=(B, H, nq),
            in_specs=[pl.BlockSpec((tq, HEAD_DIM), lambda b, h, i, bias: (b * nq + i, h)),
                      pl.BlockSpec((T, HEAD_DIM), lambda b, h, i, bias: (b, h)),
                      pl.BlockSpec((T, HEAD_DIM), lambda b, h, i, bias: (b, h)),
                      pl.BlockSpec((tq, tq), lambda b, h, i, bias: (0, 0))],
            out_specs=pl.BlockSpec((tq, HEAD_DIM), lambda b, h, i, bias: (b * nq + i, h)),
        ),
        out_shape=jax.ShapeDtypeStruct((N, HD), BF16),
        compiler_params=_cparams(("arbitrary", "arbitrary", "arbitrary")),
        name="sb_attn_prompt",
    )(bias.astype(F32), q, k, v, u)


def _sb_sample_kernel(pt_ref, qbd_ref, bias_ref, kn_ref, vn_ref, kp_ref, vp_ref, u_ref, o_ref,
                      rsum_ref, acc_ref, *, t_new, scale):
    j = pl.program_id(1)
    R = qbd_ref.shape[0]
    P = kp_ref.shape[0]

    @pl.when(j == 0)
    def _():
        rsum_ref[...] = jnp.zeros_like(rsum_ref)
        acc_ref[...] = jnp.zeros_like(acc_ref)

    def block(k_ref, v_ref, visible):
        z = _dot_nt(qbd_ref[...], k_ref[...].astype(BF16)) * scale + bias_ref[...]
        a, rsum = _sb_block(z, visible, u_ref[...], rsum_ref[...])
        rsum_ref[...] = rsum
        acc_ref[...] += _dot(a.astype(BF16), v_ref[...].astype(BF16))

    @pl.when(j == 0)
    def _():
        t_of_row = lax.broadcasted_iota(jnp.int32, (R, P), 0) % t_new
        block(kn_ref, vn_ref, lax.broadcasted_iota(jnp.int32, (R, P), 1) < t_of_row)

    @pl.when(j > 0)
    def _():
        block(kp_ref, vp_ref, None)

    @pl.when(j == pl.num_programs(1) - 1)
    def _():
        o_ref[...] = acc_ref[...]


def sb_attention_sample(q, k_new, v_new, cache_k, cache_v, page_table, bias, *, B, T):
    HD = q.shape[1]
    H = HD // HEAD_DIM
    n_pages = page_table.shape[1]
    P = cache_k.shape[1]
    R = H * T
    head_of_col = jnp.arange(HD) // HEAD_DIM
    qb = q.reshape(B, 1, T, HD) * (head_of_col[None, None, None, :] == jnp.arange(H)[None, :, None, None])
    qbd = qb.reshape(B, R, HD).astype(BF16)
    bias_rows = jnp.repeat(bias.astype(F32), T).reshape(R, 1)
    pad = lambda a: jnp.pad(a.reshape(B, T, HD), ((0, 0), (0, P - T), (0, 0)))
    ck = cache_k.reshape(-1, P, HD)
    cv = cache_v.reshape(-1, P, HD)
    u = jnp.asarray(np.triu(np.ones((P, P), np.float32), 0).T - np.eye(P, dtype=np.float32), BF16)

    def page_idx(b, j, pt):
        return (pt[b, n_pages - jnp.maximum(j, 1)], 0, 0)

    kern = functools.partial(_sb_sample_kernel, t_new=T, scale=HEAD_DIM ** -0.5)
    o_full = pl.pallas_call(
        kern,
        grid_spec=pltpu.PrefetchScalarGridSpec(
            num_scalar_prefetch=1,
            grid=(B, n_pages + 1),
            in_specs=[pl.BlockSpec((None, R, HD), lambda b, j, pt: (b, 0, 0)),
                      pl.BlockSpec((R, 1), lambda b, j, pt: (0, 0)),
                      pl.BlockSpec((None, P, HD), lambda b, j, pt: (b, 0, 0)),
                      pl.BlockSpec((None, P, HD), lambda b, j, pt: (b, 0, 0)),
                      pl.BlockSpec((None, P, HD), page_idx),
                      pl.BlockSpec((None, P, HD), page_idx),
                      pl.BlockSpec((P, P), lambda b, j, pt: (0, 0))],
            out_specs=pl.BlockSpec((None, R, HD), lambda b, j, pt: (b, 0, 0)),
            scratch_shapes=[pltpu.VMEM((R, 1), F32), pltpu.VMEM((R, HD), F32)],
        ),
        out_shape=jax.ShapeDtypeStruct((B, R, HD), F32),
        compiler_params=_cparams(("arbitrary", "arbitrary")),
        name="sb_attn_sample",
    )(page_table, qbd, bias_rows, pad(k_new), pad(v_new), ck, cv, u)
    o5 = o_full.reshape(B, H, T, H, HEAD_DIM)
    o = jnp.einsum('bhthd->bthd', o5)
    return o.reshape(B * T, HD).astype(BF16)


def _experts_kernel(be_ref, nb_ref, x_ref, wg_ref, wu_ref, wd_ref, o_ref, wgb_ref, wub_ref, wdb_ref):
    i = pl.program_id(0)
    prev = be_ref[jnp.maximum(i - 1, 0)]
    fresh = jnp.logical_or(i == 0, be_ref[i] != prev)

    @pl.when(fresh)
    def _():
        wgb_ref[...] = wg_ref[...].astype(BF16)
        wub_ref[...] = wu_ref[...].astype(BF16)
        wdb_ref[...] = wd_ref[...].astype(BF16)

    @pl.when(i < nb_ref[0])
    def _():
        x = x_ref[...]
        g = _dot(x, wgb_ref[...])
        up = _dot(x, wub_ref[...])
        a = (g * _sigmoid(g) * up).astype(BF16)
        o_ref[...] = _dot(a, wdb_ref[...]).astype(o_ref.dtype)

    @pl.when(i >= nb_ref[0])
    def _():
        o_ref[...] = jnp.zeros_like(o_ref)


def experts(x, block_e, n_used, wg, wu, wd, layer, tm, out_dtype, name):
    NR, D = x.shape
    Fd = wg.shape[-1]
    n_blocks = NR // tm
    return pl.pallas_call(
        _experts_kernel,
        grid_spec=pltpu.PrefetchScalarGridSpec(
            num_scalar_prefetch=2,
            grid=(n_blocks,),
            in_specs=[pl.BlockSpec((tm, D), lambda i, be, nb: (i, 0)),
                      pl.BlockSpec((None, None, D, Fd), lambda i, be, nb: (layer, be[i], 0, 0)),
                      pl.BlockSpec((None, None, D, Fd), lambda i, be, nb: (layer, be[i], 0, 0)),
                      pl.BlockSpec((None, None, Fd, D), lambda i, be, nb: (layer, be[i], 0, 0))],
            out_specs=pl.BlockSpec((tm, D), lambda i, be, nb: (i, 0)),
            scratch_shapes=[pltpu.VMEM((D, Fd), BF16), pltpu.VMEM((D, Fd), BF16),
                            pltpu.VMEM((Fd, D), BF16)],
        ),
        out_shape=jax.ShapeDtypeStruct((NR, D), out_dtype),
        compiler_params=_cparams(("arbitrary",)),
        name=name,
    )(block_e, n_used, x, wg, wu, wd)


def moe_ffn(h, logits, layer, router_bias, exp_w_gate, exp_w_up, exp_w_down,
            sh_w_gate, sh_w_up, sh_w_down, tm_routed, tm_shared):
    N, D = h.shape
    scores = jax.nn.sigmoid(logits)
    _, idx = lax.top_k(scores + router_bias[layer].astype(F32), TOP_K)
    wts = jnp.take_along_axis(scores, idx, axis=-1)
    wts = wts / jnp.sum(wts, axis=-1, keepdims=True) * ROUTED_SCALE

    tm = tm_routed
    NK = N * TOP_K
    n_blocks = -(-(NK + N_EXPERTS * (tm - 1)) // tm)
    member = jnp.sum(idx[:, :, None] == jnp.arange(N_EXPERTS)[None, None, :], axis=1).astype(jnp.int32)
    incl = jnp.cumsum(member, axis=0)
    counts = incl[-1]
    rank = jnp.take_along_axis(incl - member, idx, axis=1)
    padded = (counts + tm - 1) // tm * tm
    pend = jnp.cumsum(padded)
    pstart = pend - padded
    dest = pstart[idx] + rank
    tok = jnp.broadcast_to(jnp.arange(N, dtype=jnp.int32)[:, None], (N, TOP_K))
    rows = jnp.full((n_blocks * tm,), N, jnp.int32).at[dest.reshape(-1)].set(tok.reshape(-1))
    block_e = jnp.minimum(jnp.searchsorted(pend, jnp.arange(n_blocks, dtype=jnp.int32) * tm, side='right'),
                          N_EXPERTS - 1).astype(jnp.int32)
    n_used = (pend[-1] // tm).astype(jnp.int32).reshape(1)
    hp = jnp.concatenate([h, jnp.zeros((1, D), h.dtype)], axis=0)
    xg = hp[rows]
    yb = experts(xg, block_e, n_used, exp_w_gate, exp_w_up, exp_w_down, layer, tm, BF16, "experts_routed")
    routed = jnp.einsum('nkd,nk->nd', yb[dest].astype(F32), wts)

    nsb = N // tm_shared
    shared = experts(h, jnp.zeros((nsb,), jnp.int32), jnp.full((1,), nsb, jnp.int32),
                     sh_w_gate[:, None], sh_w_up[:, None], sh_w_down[:, None], layer, tm_shared, F32,
                     "experts_shared")
    return routed + shared


def trunk(x, mods, kv_mod, s0, past, p, cfg):
    B, T, D = x.shape
    N = B * T
    x = x.reshape(N, D)
    tm = cfg["tm"]
    HD = p["hg_w_out"].shape[1]
    lower = jnp.cumsum(jax.nn.softmax(p["hg_lb"].astype(F32), axis=0), axis=0)

    def split6(m):
        return [m[:, i * D:(i + 1) * D] for i in range(6)]

    sh_m, sc_m, g_m, sh_f, sc_f, g_f = split6(mods[0])
    h = norm_mod(x, p["norm_mix"][0], sh_m, sc_m, T, tm)
    qfig = mm(h, p["hg_w_in"], w_layer=0, tm=tm, name="hg_in")
    Tp = cfg["hg_t_pad"]
    if Tp != T:
        qfig = jnp.pad(qfig.reshape(B, T, -1), ((0, 0), (0, Tp - T), (0, 0))).reshape(B * Tp, -1)
    o, s_new = hgrn2(qfig, lower[0], p["hg_norm"][0], s0, B=B, T=Tp, C=cfg["hg_chunk"], tb=cfg["hg_tb"],
                     t_valid=None if Tp == T else T)
    if Tp != T:
        o = o.reshape(B, Tp, HD)[:, :T].reshape(N, HD)
    x = mm(o, p["hg_w_out"], w_layer=0, tm=tm, epilogue="resid", resid=x, gate=g_m, rows_per_group=T,
           name="hg_out")
    h, logits = norm_mod(x, p["norm_ffn"][0], sh_f, sc_f, T, tm, router_w=p["router_w"], router_layer=0)
    y = moe_ffn(h, logits, 0, p["router_bias"], p["exp_w_gate"], p["exp_w_up"], p["exp_w_down"],
                p["sh_w_gate"], p["sh_w_up"], p["sh_w_down"], cfg["tm_routed"], cfg["tm_shared"])
    x = x + jnp.repeat(g_f, T, axis=0) * y

    sh_kv, sc_kv = kv_mod[:, :D], kv_mod[:, D:]
    hk = norm_mod(x, p["kv_norm"], sh_kv, sc_kv, T, tm)
    k_new = mm(hk, p["w_kv"], n_out=HD, tm=tm, epilogue="headnorm", head_gain=p["k_norm"], name="k_proj")
    v_new = mm(hk, p["w_kv"], w_col0=HD, n_out=HD, tm=tm, name="v_proj")

    sh_m, sc_m, g_m, sh_f, sc_f, g_f = split6(mods[1])
    h = norm_mod(x, p["norm_mix"][1], sh_m, sc_m, T, tm)
    q = mm(h, p["sb_w_q"], w_layer=0, tm=tm, epilogue="headnorm", head_gain=p["q_norm"][0], name="q_proj")
    bias = p["sb_logit_bias"][0]
    if past is None:
        o = sb_attention_prompt(q, k_new, v_new, bias, B=B, T=T)
    else:
        o = sb_attention_sample(q, k_new, v_new, past[0], past[1], past[2], bias, B=B, T=T)
    x = mm(o, p["sb_w_out"], w_layer=0, tm=tm, epilogue="resid", resid=x, gate=g_m, rows_per_group=T,
           name="sb_out")
    h, logits = norm_mod(x, p["norm_ffn"][1], sh_f, sc_f, T, tm, router_w=p["router_w"], router_layer=1)
    y = moe_ffn(h, logits, 1, p["router_bias"], p["exp_w_gate"], p["exp_w_up"], p["exp_w_down"],
                p["sh_w_gate"], p["sh_w_up"], p["sh_w_down"], cfg["tm_routed"], cfg["tm_shared"])
    x = x + jnp.repeat(g_f, T, axis=0) * y
    H = HD // HEAD_DIM
    return (x.reshape(B, T, D), s_new[None], k_new.reshape(B, T, H, HEAD_DIM), v_new.reshape(B, T, H, HEAD_DIM))


PROMPT_CFG = dict(tm=512, hg_chunk=HG_CHUNK, hg_tb=256, hg_t_pad=None, tm_routed=256, tm_shared=512)
SAMPLE_CFG = dict(tm=32, hg_chunk=HG_CHUNK, hg_tb=HG_CHUNK, hg_t_pad=HG_CHUNK, tm_routed=16, tm_shared=32)
MOD_ROWS = 16


def kernel(x_prompt, x_sample, c_prompt, c_sample, state_hgrn, cache_k, cache_v, page_table, ada_w, ada_b,
           norm_mix, norm_ffn, hg_w_in, hg_lb, hg_norm, hg_w_out, kv_ada_w, kv_ada_b, kv_norm, w_kv, k_norm,
           sb_w_q, q_norm, sb_logit_bias, sb_w_out, router_w, router_bias, exp_w_gate, exp_w_up, exp_w_down,
           sh_w_gate, sh_w_up, sh_w_down):
    p = dict(ada_w=ada_w, ada_b=ada_b, norm_mix=norm_mix, norm_ffn=norm_ffn, hg_w_in=hg_w_in, hg_lb=hg_lb,
             hg_norm=hg_norm, hg_w_out=hg_w_out, kv_ada_w=kv_ada_w, kv_ada_b=kv_ada_b, kv_norm=kv_norm,
             w_kv=w_kv, k_norm=k_norm, sb_w_q=sb_w_q, q_norm=q_norm, sb_logit_bias=sb_logit_bias,
             sb_w_out=sb_w_out, router_w=router_w, router_bias=router_bias, exp_w_gate=exp_w_gate,
             exp_w_up=exp_w_up, exp_w_down=exp_w_down, sh_w_gate=sh_w_gate, sh_w_up=sh_w_up,
             sh_w_down=sh_w_down)
    Bp, Tp, _ = x_prompt.shape
    Bs = x_sample.shape[0]
    H = hg_w_out.shape[1] // HEAD_DIM
    c_all = jnp.concatenate([c_prompt, c_sample], axis=0)
    c_all = jnp.pad(c_all, ((0, MOD_ROWS - c_all.shape[0]), (0, 0)))
    mods = [mm(c_all, ada_w, w_layer=l, bias=ada_b[l], silu_in=True, tn=1024, name="ada") for l in range(2)]
    kv_mod = mm(c_all, kv_ada_w, bias=kv_ada_b, silu_in=True, tn=1024, name="ada_kv")
    rows_p = slice(0, Bp)
    rows_s = slice(Bp, Bp + Bs)
    cfg_p = dict(PROMPT_CFG, hg_t_pad=Tp)
    s0_prompt = jnp.zeros((Bp, H, HEAD_DIM, HEAD_DIM), state_hgrn.dtype)
    y_p, s_p, k_p, v_p = trunk(x_prompt, [m[rows_p] for m in mods], kv_mod[rows_p], s0_prompt, None, p, cfg_p)
    y_s, s_s, k_s, v_s = trunk(x_sample, [m[rows_s] for m in mods], kv_mod[rows_s], state_hgrn[0],
                               (cache_k, cache_v, page_table), p, SAMPLE_CFG)
    return (y_p, y_s, s_p, s_s, k_p, v_p, k_s, v_s)
```

```python
import functools
import math

import numpy as np
import jax
import jax.numpy as jnp
from jax import lax
from jax.experimental import pallas as pl
from jax.experimental.pallas import tpu as pltpu

F32 = jnp.float32
BF16 = jnp.bfloat16

LANES = 128
VMEM_LIMIT_BYTES = 52 * 1024 * 1024
NORM_EPS = 1e-6
HEAD_DIM = 128
N_EXPERTS = 64
TOP_K = 8
ROUTED_SCALE = 2.5
HG_CHUNK = 64
COMBINE_ROWS = 256
SAMPLE_Q_ROWS = 16


def _cparams(sem):
    return pltpu.CompilerParams(dimension_semantics=sem, vmem_limit_bytes=VMEM_LIMIT_BYTES)


def _sigmoid(x):
    return 1.0 / (1.0 + jnp.exp(-x))


def _dot(a, b):
    return jnp.dot(a, b, preferred_element_type=F32)


def _dot_nt(a, b):
    return lax.dot_general(a, b, (((1,), (1,)), ((), ())), preferred_element_type=F32)


def _dot_tn(a, b):
    return lax.dot_general(a, b, (((0,), (0,)), ((), ())), preferred_element_type=F32)


def _norm_mod_kernel(x_ref, g_ref, sh_ref, sc_ref, *rest, with_router):
    x = x_ref[...]
    ms = jnp.mean(x * x, axis=-1, keepdims=True)
    y = x * lax.rsqrt(ms + NORM_EPS) * g_ref[...]
    h = y * (1.0 + sc_ref[...]) + sh_ref[...]
    if with_router:
        rw_ref, rb_ref, o_ref, idx_ref, wt_ref = rest
        logits = jnp.dot(h, rw_ref[...], preferred_element_type=F32, precision=lax.Precision.HIGHEST)
        scores = _sigmoid(logits)
        sel = scores + rb_ref[...]
        tm, E = sel.shape
        lane = lax.broadcasted_iota(jnp.int32, (tm, E), 1)
        slot = lax.broadcasted_iota(jnp.int32, (tm, TOP_K), 1)
        idx = jnp.zeros((tm, TOP_K), jnp.int32)
        wts = jnp.zeros((tm, TOP_K), F32)
        for k in range(TOP_K):
            best = jnp.max(sel, axis=-1, keepdims=True)
            pick = jnp.min(jnp.where(sel == best, lane, E), axis=-1, keepdims=True)
            hit = lane == pick
            w_k = jnp.sum(jnp.where(hit, scores, 0.0), axis=-1, keepdims=True)
            sel = jnp.where(hit, -jnp.inf, sel)
            idx = jnp.where(slot == k, pick, idx)
            wts = jnp.where(slot == k, w_k, wts)
        idx_ref[...] = idx
        wt_ref[...] = wts / jnp.sum(wts, axis=-1, keepdims=True) * ROUTED_SCALE
    else:
        o_ref, = rest
    o_ref[...] = h.astype(o_ref.dtype)


def norm_mod(x, gain, shift, scale, rows_per_group, tm, router_w=None, router_bias=None, router_layer=None):
    N, D = x.shape
    tm = min(tm, N)
    with_router = router_w is not None
    if rows_per_group % tm == 0:
        bpg = rows_per_group // tm
        sh = shift.reshape(-1, 1, D)
        sc = scale.reshape(-1, 1, D)
        mod_spec = pl.BlockSpec((None, 1, D), lambda i: (i // bpg, 0, 0))
    else:
        sh = jnp.repeat(shift, rows_per_group, axis=0)
        sc = jnp.repeat(scale, rows_per_group, axis=0)
        mod_spec = pl.BlockSpec((tm, D), lambda i: (i, 0))
    args = [x, gain.reshape(1, D), sh, sc]
    in_specs = [pl.BlockSpec((tm, D), lambda i: (i, 0)),
                pl.BlockSpec((1, D), lambda i: (0, 0)),
                mod_spec, mod_spec]
    out_specs = pl.BlockSpec((tm, D), lambda i: (i, 0))
    out_shape = jax.ShapeDtypeStruct((N, D), BF16)
    if with_router:
        E = router_w.shape[-1]
        args += [router_w, router_bias.astype(F32).reshape(-1, 1, E)]
        in_specs += [pl.BlockSpec((None, D, E), lambda i: (router_layer, 0, 0)),
                     pl.BlockSpec((None, 1, E), lambda i: (router_layer, 0, 0))]
        out_specs = [out_specs, pl.BlockSpec((tm, TOP_K), lambda i: (i, 0)),
                     pl.BlockSpec((tm, TOP_K), lambda i: (i, 0))]
        out_shape = [out_shape, jax.ShapeDtypeStruct((N, TOP_K), jnp.int32),
                     jax.ShapeDtypeStruct((N, TOP_K), F32)]
    return pl.pallas_call(
        functools.partial(_norm_mod_kernel, with_router=with_router),
        grid=(N // tm,),
        in_specs=in_specs,
        out_specs=out_specs,
        out_shape=out_shape,
        compiler_params=_cparams(("arbitrary",)),
        name="norm_router" if with_router else "norm_mod",
    )(*args)


def _mm_kernel(*refs, silu_in, has_bias, epilogue, tn):
    it = iter(refs)
    x_ref = next(it)
    w_ref = next(it)
    b_ref = next(it) if has_bias else None
    if epilogue == "headnorm":
        hg_ref = next(it)
    elif epilogue == "resid":
        r_ref = next(it)
        gate_ref = next(it)
    o_ref = next(it)
    wbf_ref = next(it)

    @pl.when(pl.program_id(1) == 0)
    def _():
        wbf_ref[...] = w_ref[...].astype(BF16)

    x = x_ref[...]
    if silu_in:
        x = x.astype(F32)
        x = x * _sigmoid(x)
    y = _dot(x.astype(BF16), wbf_ref[...])
    if has_bias:
        y = y + b_ref[...]
    if epilogue == "headnorm":
        gain = hg_ref[...]
        for c in range(tn // HEAD_DIM):
            sl = slice(c * HEAD_DIM, (c + 1) * HEAD_DIM)
            yc = y[:, sl]
            ms = jnp.mean(yc * yc, axis=-1, keepdims=True)
            o_ref[:, sl] = (yc * lax.rsqrt(ms + NORM_EPS) * gain).astype(o_ref.dtype)
    elif epilogue == "resid":
        o_ref[...] = (r_ref[...] + gate_ref[...] * y).astype(o_ref.dtype)
    else:
        o_ref[...] = y.astype(o_ref.dtype)


def mm(x, w, *, w_layer=None, w_col0=0, n_out=None, bias=None, silu_in=False, out_dtype=F32,
       tm=512, tn=512, epilogue=None, head_gain=None, resid=None, gate=None, rows_per_group=None,
       name="mm"):
    M, K = x.shape
    Ntot = w.shape[-1]
    N = Ntot - w_col0 if n_out is None else n_out
    tm = min(tm, M)
    tn = min(tn, N)
    assert M % tm == 0 and N % tn == 0 and w_col0 % tn == 0
    c0 = w_col0 // tn
    if w.ndim == 3:
        w_spec = pl.BlockSpec((None, K, tn), lambda j, i: (w_layer, 0, j + c0))
    else:
        w_spec = pl.BlockSpec((K, tn), lambda j, i: (0, j + c0))
    args = [x, w]
    in_specs = [pl.BlockSpec((tm, K), lambda j, i: (i, 0)), w_spec]
    if bias is not None:
        args.append(bias.reshape(1, -1))
        in_specs.append(pl.BlockSpec((1, tn), lambda j, i: (0, j + c0)))
    if epilogue == "headnorm":
        args.append(head_gain.reshape(1, HEAD_DIM))
        in_specs.append(pl.BlockSpec((1, HEAD_DIM), lambda j, i: (0, 0)))
    elif epilogue == "resid":
        args.append(resid)
        in_specs.append(pl.BlockSpec((tm, tn), lambda j, i: (i, j)))
        if rows_per_group % tm == 0:
            bpg = rows_per_group // tm
            args.append(gate.reshape(-1, 1, N))
            in_specs.append(pl.BlockSpec((None, 1, tn), lambda j, i: (i // bpg, 0, j)))
        else:
            args.append(jnp.repeat(gate, rows_per_group, axis=0))
            in_specs.append(pl.BlockSpec((tm, tn), lambda j, i: (i, j)))
    kern = functools.partial(_mm_kernel, silu_in=silu_in, has_bias=bias is not None,
                             epilogue=epilogue, tn=tn)
    return pl.pallas_call(
        kern,
        grid=(N // tn, M // tm),
        in_specs=in_specs,
        out_specs=pl.BlockSpec((tm, tn), lambda j, i: (i, j)),
        out_shape=jax.ShapeDtypeStruct((M, N), out_dtype),
        scratch_shapes=[pltpu.VMEM((K, tn), BF16)],
        compiler_params=_cparams(("arbitrary", "arbitrary")),
        name=name,
    )(*args)


def _hgrn_constants(C):
    r = np.arange(C)
    mats = [(r[None, :] <= r[:, None]).astype(np.float32)]
    masks = [np.eye(C, dtype=np.float32)]
    m = C // 2
    while m >= 1:
        base = (r // (2 * m)) * (2 * m)
        pivot = base + m - 1
        upper = r > pivot
        j = r[None, :]
        mat = np.where(upper[:, None], (j > pivot[:, None]) & (j <= r[:, None]),
                       (j > r[:, None]) & (j <= pivot[:, None]))
        mats.append(mat.astype(np.float32))
        same = base[:, None] == base[None, :]
        masks.append((same & upper[:, None] & (~upper)[None, :]).astype(np.float32))
        m //= 2
    return np.concatenate(mats, axis=0), np.stack(masks, axis=0)


def _hgrn_kernel(q_ref, fu_ref, i_ref, g_ref, lb_ref, gain_ref, s0_ref, mat_ref, mask_ref,
                 o_ref, sout_ref, s_ref, *, C, n_chunks, n_levels, t_valid):
    t = pl.program_id(2)

    @pl.when(t == 0)
    def _():
        s_ref[...] = s0_ref[...].T

    lb = lb_ref[...]
    gain = gain_ref[...]
    staged = []
    for c in range(n_chunks):
        rows = slice(c * C, (c + 1) * C)
        q = q_ref[rows, :]
        f = lb + (1.0 - lb) * _sigmoid(fu_ref[rows, :])
        logf = jnp.log(f)
        kk = 1.0 - f
        if t_valid is not None:
            valid = (lax.broadcasted_iota(jnp.int32, (C, HEAD_DIM), 0) + c * C) < t_valid
            logf = jnp.where(valid, logf, 0.0)
            kk = jnp.where(valid, kk, 0.0)
        hi = logf.astype(BF16)
        lo = (logf - hi.astype(F32)).astype(BF16)
        e = _dot(mat_ref[...], hi) + _dot(mat_ref[...], lo)
        b = e[0:C, :]
        scores = mask_ref[0] * _dot_nt(q.astype(BF16), kk.astype(BF16))
        for l in range(1, n_levels + 1):
            x = jnp.exp(e[l * C:(l + 1) * C, :])
            scores = scores + mask_ref[l] * _dot_nt((q * x).astype(BF16), (kk * x).astype(BF16))
        b_end = b[C - 1:C, :]
        staged.append(((q * jnp.exp(b)).astype(BF16), scores.astype(BF16), i_ref[rows, :].astype(BF16),
                       (kk * jnp.exp(b_end - b)).astype(BF16), jnp.exp(b_end)))

    St = s_ref[...]
    for c, (q_dec, scores, vb, khat, decay) in enumerate(staged):
        rows = slice(c * C, (c + 1) * C)
        o = _dot_nt(q_dec, St.astype(BF16)) + _dot(scores, vb)
        St = decay * St + _dot_tn(vb, khat)
        ms = jnp.mean(o * o, axis=-1, keepdims=True)
        gt = g_ref[rows, :]
        o_ref[rows, :] = (o * lax.rsqrt(ms + NORM_EPS) * gain * (gt * _sigmoid(gt))).astype(o_ref.dtype)
    s_ref[...] = St

    @pl.when(t == pl.num_programs(2) - 1)
    def _():
        sout_ref[...] = s_ref[...].T


def hgrn2(qfig, lb, gain, s0, *, B, T, C, tb, t_valid=None):
    H = s0.shape[1]
    HD = H * HEAD_DIM
    x3 = qfig.reshape(B, T, 4 * HD)
    mats, masks = _hgrn_constants(C)
    n_levels = masks.shape[0] - 1
    kern = functools.partial(_hgrn_kernel, C=C, n_chunks=tb // C, n_levels=n_levels, t_valid=t_valid)

    def col(off):
        return pl.BlockSpec((None, tb, HEAD_DIM), lambda b, h, t: (b, t, h + off * H))

    head_vec = pl.BlockSpec((1, HEAD_DIM), lambda b, h, t: (0, h))
    o, s_out = pl.pallas_call(
        kern,
        grid=(B, H, T // tb),
        in_specs=[col(0), col(1), col(2), col(3), head_vec, head_vec,
                  pl.BlockSpec((None, None, HEAD_DIM, HEAD_DIM), lambda b, h, t: (b, h, 0, 0)),
                  pl.BlockSpec(mats.shape, lambda b, h, t: (0, 0)),
                  pl.BlockSpec(masks.shape, lambda b, h, t: (0, 0, 0))],
        out_specs=[pl.BlockSpec((None, tb, HEAD_DIM), lambda b, h, t: (b, t, h)),
                   pl.BlockSpec((None, None, HEAD_DIM, HEAD_DIM), lambda b, h, t: (b, h, 0, 0))],
        out_shape=[jax.ShapeDtypeStruct((B, T, HD), BF16),
                   jax.ShapeDtypeStruct(s0.shape, F32)],
        scratch_shapes=[pltpu.VMEM((HEAD_DIM, HEAD_DIM), F32)],
        compiler_params=_cparams(("arbitrary", "arbitrary", "arbitrary")),
        name="hgrn2",
    )(x3, x3, x3, x3, lb.reshape(1, HD), gain.reshape(1, HD), s0,
      jnp.asarray(mats, BF16), jnp.asarray(masks))
    return o.reshape(B * T, HD), s_out


def _log_sig_pair(z):
    lk = -jnp.maximum(z, 0.0) - jnp.log1p(jnp.exp(-jnp.abs(z)))
    return z + lk, lk


def _suffix_sum(lk, u):
    hi = lk.astype(BF16)
    lo = (lk - hi.astype(F32)).astype(BF16)
    return _dot(hi, u) + _dot(lo, u)


def _sb_block(z, visible, u, rsum):
    ls, lk = _log_sig_pair(z)
    if visible is not None:
        lk = jnp.where(visible, lk, 0.0)
    la = _suffix_sum(lk, u) + rsum
    a = jnp.exp(ls + la)
    if visible is not None:
        a = jnp.where(visible, a, 0.0)
    return a, rsum + jnp.sum(lk, axis=-1, keepdims=True)


def _sb_prompt_kernel(bias_ref, q_ref, k_ref, v_ref, u_ref, o_ref, *, tq, tk, scale):
    h = pl.program_id(1)
    qi = pl.program_id(2)
    bias = bias_ref[h]
    q = q_ref[...].astype(BF16)
    u = u_ref[...]
    nd = tq // tk

    def block(kb, visible, rsum, acc):
        start = pl.multiple_of(kb * tk, tk)
        k = k_ref[pl.ds(start, tk), :].astype(BF16)
        v = v_ref[pl.ds(start, tk), :].astype(BF16)
        z = _dot_nt(q, k) * scale + bias
        a, rsum = _sb_block(z, visible, u, rsum)
        return rsum, acc + _dot(a.astype(BF16), v)

    row = lax.broadcasted_iota(jnp.int32, (tq, tk), 0)
    col = lax.broadcasted_iota(jnp.int32, (tq, tk), 1)
    carry = (jnp.zeros((tq, 1), F32), jnp.zeros((tq, HEAD_DIM), F32))
    for d in reversed(range(nd)):
        carry = block(qi * nd + d, col + d * tk < row, *carry)

    def body(it, carry):
        for d in range(nd):
            carry = block((qi - it) * nd - 1 - d, None, *carry)
        return carry

    rsum, acc = lax.fori_loop(0, qi, body, carry)
    o_ref[...] = acc.astype(o_ref.dtype)


def sb_attention_prompt(q, k, v, bias, *, B, T, tq=512, tk=256):
    N, HD = q.shape
    H = HD // HEAD_DIM
    tq = min(tq, T)
    tk = min(tk, tq)
    nq = T // tq
    u = jnp.asarray(np.triu(np.ones((tk, tk), np.float32), 0).T - np.eye(tk, dtype=np.float32), BF16)
    kern = functools.partial(_sb_prompt_kernel, tq=tq, tk=tk, scale=HEAD_DIM ** -0.5)
    return pl.pallas_call(
        kern,
        grid_spec=pltpu.PrefetchScalarGridSpec(
            num_scalar_prefetch=1,
            grid=(B, H, nq),
            in_specs=[pl.BlockSpec((tq, HEAD_DIM), lambda b, h, i, bias: (b * nq + i, h)),
                      pl.BlockSpec((T, HEAD_DIM), lambda b, h, i, bias: (b, h)),
                      pl.BlockSpec((T, HEAD_DIM), lambda b, h, i, bias: (b, h)),
                      pl.BlockSpec((tk, tk), lambda b, h, i, bias: (0, 0))],
            out_specs=pl.BlockSpec((tq, HEAD_DIM), lambda b, h, i, bias: (b * nq + i, h)),
        ),
        out_shape=jax.ShapeDtypeStruct((N, HD), BF16),
        compiler_params=_cparams(("arbitrary", "arbitrary", "arbitrary")),
        name="sb_attn_prompt",
    )(bias.astype(F32), q, k, v, u)


def _sb_sample_kernel(pt_ref, q_ref, bias_ref, kn_ref, vn_ref, kp_ref, vp_ref, u_ref, o_ref,
                      rsum_ref, acc_ref, *, scale):
    j = pl.program_id(1)
    H, tp, _ = q_ref.shape
    P = kp_ref.shape[0] // H
    R = H * tp

    @pl.when(j == 0)
    def _():
        rsum_ref[...] = jnp.zeros_like(rsum_ref)
        acc_ref[...] = jnp.zeros_like(acc_ref)

    def head_rows(ref, h):
        return ref[pl.ds(h, P, stride=H), :].astype(BF16)

    def block(k_ref, v_ref, visible):
        z = jnp.concatenate([_dot_nt(q_ref[h], head_rows(k_ref, h)) for h in range(H)], axis=0)
        z = z * scale + bias_ref[...]
        a, rsum = _sb_block(z, visible, u_ref[...], rsum_ref[...])
        rsum_ref[...] = rsum
        for h in range(H):
            acc_ref[h] += _dot(a[h * tp:(h + 1) * tp, :].astype(BF16), head_rows(v_ref, h))

    @pl.when(j == 0)
    def _():
        t_of_row = lax.broadcasted_iota(jnp.int32, (R, P), 0) % tp
        block(kn_ref, vn_ref, lax.broadcasted_iota(jnp.int32, (R, P), 1) < t_of_row)

    @pl.when(j > 0)
    def _():
        block(kp_ref, vp_ref, None)

    @pl.when(j == pl.num_programs(1) - 1)
    def _():
        o_ref[...] = acc_ref[...]


def sb_attention_sample(q, k_new, v_new, cache_k, cache_v, page_table, bias, *, B, T):
    HD = q.shape[1]
    H = HD // HEAD_DIM
    n_pages = page_table.shape[1]
    P = cache_k.shape[1]
    tp = SAMPLE_Q_ROWS
    R = H * tp
    qh = jnp.pad(q.reshape(B, T, H, HEAD_DIM).transpose(0, 2, 1, 3), ((0, 0), (0, 0), (0, tp - T), (0, 0)))
    bias_rows = jnp.repeat(bias.astype(F32), tp).reshape(R, 1)
    pad = lambda a: jnp.pad(a.reshape(B, T * H, HEAD_DIM), ((0, 0), (0, (P - T) * H), (0, 0)))
    u = jnp.asarray(np.triu(np.ones((P, P), np.float32), 0).T - np.eye(P, dtype=np.float32), BF16)

    def page_idx(b, j, pt):
        return (pt[b, n_pages - jnp.maximum(j, 1)], 0, 0)

    page = (None, P * H, HEAD_DIM)
    o = pl.pallas_call(
        functools.partial(_sb_sample_kernel, scale=HEAD_DIM ** -0.5),
        grid_spec=pltpu.PrefetchScalarGridSpec(
            num_scalar_prefetch=1,
            grid=(B, n_pages + 1),
            in_specs=[pl.BlockSpec((None, H, tp, HEAD_DIM), lambda b, j, pt: (b, 0, 0, 0)),
                      pl.BlockSpec((R, 1), lambda b, j, pt: (0, 0)),
                      pl.BlockSpec(page, lambda b, j, pt: (b, 0, 0)),
                      pl.BlockSpec(page, lambda b, j, pt: (b, 0, 0)),
                      pl.BlockSpec(page, page_idx),
                      pl.BlockSpec(page, page_idx),
                      pl.BlockSpec((P, P), lambda b, j, pt: (0, 0))],
            out_specs=pl.BlockSpec((None, H, tp, HEAD_DIM), lambda b, j, pt: (b, 0, 0, 0)),
            scratch_shapes=[pltpu.VMEM((R, 1), F32), pltpu.VMEM((H, tp, HEAD_DIM), F32)],
        ),
        out_shape=jax.ShapeDtypeStruct((B, H, tp, HEAD_DIM), F32),
        compiler_params=_cparams(("arbitrary", "arbitrary")),
        name="sb_attn_sample",
    )(page_table, qh.astype(BF16), bias_rows, pad(k_new), pad(v_new),
      cache_k.reshape(-1, P * H, HEAD_DIM), cache_v.reshape(-1, P * H, HEAD_DIM), u)
    return o[:, :, :T].transpose(0, 2, 1, 3).reshape(B * T, HD).astype(BF16)


def _experts_kernel(be_ref, nb_ref, x_ref, wg_ref, wu_ref, wd_ref, o_ref, wgb_ref, wub_ref, wdb_ref):
    i = pl.program_id(0)
    prev = be_ref[jnp.maximum(i - 1, 0)]
    fresh = jnp.logical_or(i == 0, be_ref[i] != prev)

    @pl.when(fresh)
    def _():
        wgb_ref[...] = wg_ref[...].astype(BF16)
        wub_ref[...] = wu_ref[...].astype(BF16)
        wdb_ref[...] = wd_ref[...].astype(BF16)

    @pl.when(i < nb_ref[0])
    def _():
        x = x_ref[...]
        g = _dot(x, wgb_ref[...])
        up = _dot(x, wub_ref[...])
        a = (g * _sigmoid(g) * up).astype(BF16)
        o_ref[...] = _dot(a, wdb_ref[...]).astype(o_ref.dtype)

    @pl.when(i >= nb_ref[0])
    def _():
        o_ref[...] = jnp.zeros_like(o_ref)


def experts(x, block_e, n_used, wg, wu, wd, layer, tm, out_dtype, name):
    NR, D = x.shape
    Fd = wg.shape[-1]
    n_blocks = NR // tm
    return pl.pallas_call(
        _experts_kernel,
        grid_spec=pltpu.PrefetchScalarGridSpec(
            num_scalar_prefetch=2,
            grid=(n_blocks,),
            in_specs=[pl.BlockSpec((tm, D), lambda i, be, nb: (i, 0)),
                      pl.BlockSpec((None, None, D, Fd), lambda i, be, nb: (layer, be[i], 0, 0)),
                      pl.BlockSpec((None, None, D, Fd), lambda i, be, nb: (layer, be[i], 0, 0)),
                      pl.BlockSpec((None, None, Fd, D), lambda i, be, nb: (layer, be[i], 0, 0))],
            out_specs=pl.BlockSpec((tm, D), lambda i, be, nb: (i, 0)),
            scratch_shapes=[pltpu.VMEM((D, Fd), BF16), pltpu.VMEM((D, Fd), BF16),
                            pltpu.VMEM((Fd, D), BF16)],
        ),
        out_shape=jax.ShapeDtypeStruct((NR, D), out_dtype),
        compiler_params=_cparams(("arbitrary",)),
        name=name,
    )(block_e, n_used, x, wg, wu, wd)


def _combine_kernel(x_ref, gate_ref, sh_ref, wt_ref, g_ref, o_ref, *, D):
    wts = wt_ref[...]
    y = sh_ref[...]
    for k in range(TOP_K):
        y = y + wts[:, k:k + 1] * g_ref[:, k * D:(k + 1) * D].astype(F32)
    o_ref[...] = x_ref[...] + gate_ref[...] * y


def combine(x, gate, shared, wts, gathered, rows_per_group, tm):
    N, D = x.shape
    tm = min(tm, N)
    if rows_per_group % tm == 0:
        bpg = rows_per_group // tm
        gate = gate.reshape(-1, 1, D)
        gate_spec = pl.BlockSpec((None, 1, D), lambda i: (i // bpg, 0, 0))
    else:
        gate = jnp.repeat(gate, rows_per_group, axis=0)
        gate_spec = pl.BlockSpec((tm, D), lambda i: (i, 0))
    row = pl.BlockSpec((tm, D), lambda i: (i, 0))
    return pl.pallas_call(
        functools.partial(_combine_kernel, D=D),
        grid=(N // tm,),
        in_specs=[row, gate_spec, row, pl.BlockSpec((tm, TOP_K), lambda i: (i, 0)),
                  pl.BlockSpec((tm, TOP_K * D), lambda i: (i, 0))],
        out_specs=row,
        out_shape=jax.ShapeDtypeStruct((N, D), F32),
        compiler_params=_cparams(("arbitrary",)),
        name="moe_combine",
    )(x, gate, shared, wts, gathered)


def moe_ffn(x, gate, rows_per_group, h, idx, wts, layer, exp_w_gate, exp_w_up, exp_w_down,
            sh_w_gate, sh_w_up, sh_w_down, tm_routed, tm_shared):
    N, D = h.shape

    tm = tm_routed
    NK = N * TOP_K
    n_blocks = -(-(NK + N_EXPERTS * (tm - 1)) // tm)
    member = jnp.sum(idx[:, :, None] == jnp.arange(N_EXPERTS)[None, None, :], axis=1).astype(jnp.int32)
    incl = jnp.cumsum(member, axis=0)
    counts = incl[-1]
    rank = jnp.take_along_axis(incl - member, idx, axis=1)
    padded = (counts + tm - 1) // tm * tm
    pend = jnp.cumsum(padded)
    pstart = pend - padded
    dest = pstart[idx] + rank
    tok = jnp.broadcast_to(jnp.arange(N, dtype=jnp.int32)[:, None], (N, TOP_K))
    rows = jnp.full((n_blocks * tm,), N, jnp.int32).at[dest.reshape(-1)].set(tok.reshape(-1))
    block_e = jnp.minimum(jnp.searchsorted(pend, jnp.arange(n_blocks, dtype=jnp.int32) * tm, side='right'),
                          N_EXPERTS - 1).astype(jnp.int32)
    n_used = (pend[-1] // tm).astype(jnp.int32).reshape(1)
    hp = jnp.concatenate([h, jnp.zeros((1, D), h.dtype)], axis=0)
    xg = hp[rows]
    yb = experts(xg, block_e, n_used, exp_w_gate, exp_w_up, exp_w_down, layer, tm, BF16, "experts_routed")
    gathered = yb[dest.reshape(-1)].reshape(N, TOP_K * D)

    nsb = N // tm_shared
    shared = experts(h, jnp.zeros((nsb,), jnp.int32), jnp.full((1,), nsb, jnp.int32),
                     sh_w_gate[:, None], sh_w_up[:, None], sh_w_down[:, None], layer, tm_shared, F32,
                     "experts_shared")
    return combine(x, gate, shared, wts, gathered, rows_per_group, COMBINE_ROWS)


def trunk(x, mods, kv_mod, s0, past, p, cfg):
    B, T, D = x.shape
    N = B * T
    x = x.reshape(N, D)
    tm = cfg["tm"]
    HD = p["hg_w_out"].shape[1]
    lower = jnp.cumsum(jax.nn.softmax(p["hg_lb"].astype(F32), axis=0), axis=0)

    def split6(m):
        return [m[:, i * D:(i + 1) * D] for i in range(6)]

    sh_m, sc_m, g_m, sh_f, sc_f, g_f = split6(mods[0])
    h = norm_mod(x, p["norm_mix"][0], sh_m, sc_m, T, tm)
    qfig = mm(h, p["hg_w_in"], w_layer=0, tm=tm, name="hg_in")
    Tp = cfg["hg_t_pad"]
    if Tp != T:
        qfig = jnp.pad(qfig.reshape(B, T, -1), ((0, 0), (0, Tp - T), (0, 0))).reshape(B * Tp, -1)
    o, s_new = hgrn2(qfig, lower[0], p["hg_norm"][0], s0, B=B, T=Tp, C=cfg["hg_chunk"], tb=cfg["hg_tb"],
                     t_valid=None if Tp == T else T)
    if Tp != T:
        o = o.reshape(B, Tp, HD)[:, :T].reshape(N, HD)
    x = mm(o, p["hg_w_out"], w_layer=0, tm=tm, epilogue="resid", resid=x, gate=g_m, rows_per_group=T,
           name="hg_out")
    h, idx, wts = norm_mod(x, p["norm_ffn"][0], sh_f, sc_f, T, tm, router_w=p["router_w"],
                           router_bias=p["router_bias"], router_layer=0)
    x = moe_ffn(x, g_f, T, h, idx, wts, 0, p["exp_w_gate"], p["exp_w_up"], p["exp_w_down"],
                p["sh_w_gate"], p["sh_w_up"], p["sh_w_down"], cfg["tm_routed"], cfg["tm_shared"])

    sh_kv, sc_kv = kv_mod[:, :D], kv_mod[:, D:]
    hk = norm_mod(x, p["kv_norm"], sh_kv, sc_kv, T, tm)
    k_new = mm(hk, p["w_kv"], n_out=HD, tm=tm, epilogue="headnorm", head_gain=p["k_norm"], name="k_proj")
    v_new = mm(hk, p["w_kv"], w_col0=HD, n_out=HD, tm=tm, name="v_proj")

    sh_m, sc_m, g_m, sh_f, sc_f, g_f = split6(mods[1])
    h = norm_mod(x, p["norm_mix"][1], sh_m, sc_m, T, tm)
    q = mm(h, p["sb_w_q"], w_layer=0, tm=tm, epilogue="headnorm", head_gain=p["q_norm"][0], name="q_proj")
    bias = p["sb_logit_bias"][0]
    if past is None:
        o = sb_attention_prompt(q, k_new, v_new, bias, B=B, T=T)
    else:
        o = sb_attention_sample(q, k_new, v_new, past[0], past[1], past[2], bias, B=B, T=T)
    x = mm(o, p["sb_w_out"], w_layer=0, tm=tm, epilogue="resid", resid=x, gate=g_m, rows_per_group=T,
           name="sb_out")
    h, idx, wts = norm_mod(x, p["norm_ffn"][1], sh_f, sc_f, T, tm, router_w=p["router_w"],
                           router_bias=p["router_bias"], router_layer=1)
    x = moe_ffn(x, g_f, T, h, idx, wts, 1, p["exp_w_gate"], p["exp_w_up"], p["exp_w_down"],
                p["sh_w_gate"], p["sh_w_up"], p["sh_w_down"], cfg["tm_routed"], cfg["tm_shared"])
    H = HD // HEAD_DIM
    return (x.reshape(B, T, D), s_new[None], k_new.reshape(B, T, H, HEAD_DIM), v_new.reshape(B, T, H, HEAD_DIM))


PROMPT_CFG = dict(tm=512, hg_chunk=HG_CHUNK, hg_tb=256, hg_t_pad=None, tm_routed=256, tm_shared=512)
SAMPLE_CFG = dict(tm=32, hg_chunk=HG_CHUNK, hg_tb=HG_CHUNK, hg_t_pad=HG_CHUNK, tm_routed=16, tm_shared=32)
MOD_ROWS = 16


def kernel(x_prompt, x_sample, c_prompt, c_sample, state_hgrn, cache_k, cache_v, page_table, ada_w, ada_b,
           norm_mix, norm_ffn, hg_w_in, hg_lb, hg_norm, hg_w_out, kv_ada_w, kv_ada_b, kv_norm, w_kv, k_norm,
           sb_w_q, q_norm, sb_logit_bias, sb_w_out, router_w, router_bias, exp_w_gate, exp_w_up, exp_w_down,
           sh_w_gate, sh_w_up, sh_w_down):
    p = dict(ada_w=ada_w, ada_b=ada_b, norm_mix=norm_mix, norm_ffn=norm_ffn, hg_w_in=hg_w_in, hg_lb=hg_lb,
             hg_norm=hg_norm, hg_w_out=hg_w_out, kv_ada_w=kv_ada_w, kv_ada_b=kv_ada_b, kv_norm=kv_norm,
             w_kv=w_kv, k_norm=k_norm, sb_w_q=sb_w_q, q_norm=q_norm, sb_logit_bias=sb_logit_bias,
             sb_w_out=sb_w_out, router_w=router_w, router_bias=router_bias, exp_w_gate=exp_w_gate,
             exp_w_up=exp_w_up, exp_w_down=exp_w_down, sh_w_gate=sh_w_gate, sh_w_up=sh_w_up,
             sh_w_down=sh_w_down)
    Bp, Tp, _ = x_prompt.shape
    Bs = x_sample.shape[0]
    H = hg_w_out.shape[1] // HEAD_DIM
    c_all = jnp.concatenate([c_prompt, c_sample], axis=0)
    c_all = jnp.pad(c_all, ((0, MOD_ROWS - c_all.shape[0]), (0, 0)))
    mods = [mm(c_all, ada_w, w_layer=l, bias=ada_b[l], silu_in=True, tn=1024, name="ada") for l in range(2)]
    kv_mod = mm(c_all, kv_ada_w, bias=kv_ada_b, silu_in=True, tn=1024, name="ada_kv")
    rows_p = slice(0, Bp)
    rows_s = slice(Bp, Bp + Bs)
    cfg_p = dict(PROMPT_CFG, hg_t_pad=Tp)
    s0_prompt = jnp.zeros((Bp, H, HEAD_DIM, HEAD_DIM), state_hgrn.dtype)
    y_p, s_p, k_p, v_p = trunk(x_prompt, [m[rows_p] for m in mods], kv_mod[rows_p], s0_prompt, None, p, cfg_p)
    y_s, s_s, k_s, v_s = trunk(x_sample, [m[rows_s] for m in mods], kv_mod[rows_s], state_hgrn[0],
                               (cache_k, cache_v, page_table), p, SAMPLE_CFG)
    return (y_p, y_s, s_p, s_s, k_p, v_p, k_s, v_s)
```

```python
import functools
import math

import numpy as np
import jax
import jax.numpy as jnp
from jax import lax
from jax.experimental import pallas as pl
from jax.experimental.pallas import tpu as pltpu

F32 = jnp.float32
BF16 = jnp.bfloat16

LANES = 128
SUBLANES = 8
SAMPLE_PAGES_PER_STEP = 4
VMEM_LIMIT_BYTES = 52 * 1024 * 1024
NORM_EPS = 1e-6
HEAD_DIM = 128
N_EXPERTS = 64
TOP_K = 8
ROUTED_SCALE = 2.5
HG_CHUNK = 64
COMBINE_ROWS = 256
SAMPLE_Q_ROWS = 16


def _cparams(sem):
    return pltpu.CompilerParams(dimension_semantics=sem, vmem_limit_bytes=VMEM_LIMIT_BYTES)


def _sigmoid(x):
    return 1.0 / (1.0 + jnp.exp(-x))


def _dot(a, b):
    return jnp.dot(a, b, preferred_element_type=F32)


def _dot_nt(a, b):
    return lax.dot_general(a, b, (((1,), (1,)), ((), ())), preferred_element_type=F32)


def _dot_tn(a, b):
    return lax.dot_general(a, b, (((0,), (0,)), ((), ())), preferred_element_type=F32)


def _norm_mod_kernel(x_ref, g_ref, sh_ref, sc_ref, *rest, with_router):
    x = x_ref[...]
    ms = jnp.mean(x * x, axis=-1, keepdims=True)
    y = x * lax.rsqrt(ms + NORM_EPS) * g_ref[...]
    h = y * (1.0 + sc_ref[...]) + sh_ref[...]
    if with_router:
        rw_ref, rb_ref, o_ref, idx_ref, wt_ref = rest
        logits = jnp.dot(h, rw_ref[...], preferred_element_type=F32, precision=lax.Precision.HIGHEST)
        scores = _sigmoid(logits)
        sel = scores + rb_ref[...]
        tm, E = sel.shape
        lane = lax.broadcasted_iota(jnp.int32, (tm, E), 1)
        slot = lax.broadcasted_iota(jnp.int32, (tm, TOP_K), 1)
        idx = jnp.zeros((tm, TOP_K), jnp.int32)
        wts = jnp.zeros((tm, TOP_K), F32)
        for k in range(TOP_K):
            best = jnp.max(sel, axis=-1, keepdims=True)
            pick = jnp.min(jnp.where(sel == best, lane, E), axis=-1, keepdims=True)
            hit = lane == pick
            w_k = jnp.sum(jnp.where(hit, scores, 0.0), axis=-1, keepdims=True)
            sel = jnp.where(hit, -jnp.inf, sel)
            idx = jnp.where(slot == k, pick, idx)
            wts = jnp.where(slot == k, w_k, wts)
        idx_ref[...] = idx
        wt_ref[...] = wts / jnp.sum(wts, axis=-1, keepdims=True) * ROUTED_SCALE
    else:
        o_ref, = rest
    o_ref[...] = h.astype(o_ref.dtype)


def norm_mod(x, gain, shift, scale, rows_per_group, tm, router_w=None, router_bias=None, router_layer=None):
    N, D = x.shape
    tm = min(tm, N)
    with_router = router_w is not None
    if rows_per_group % tm == 0:
        bpg = rows_per_group // tm
        sh = shift.reshape(-1, 1, D)
        sc = scale.reshape(-1, 1, D)
        mod_spec = pl.BlockSpec((None, 1, D), lambda i: (i // bpg, 0, 0))
    else:
        sh = jnp.repeat(shift, rows_per_group, axis=0)
        sc = jnp.repeat(scale, rows_per_group, axis=0)
        mod_spec = pl.BlockSpec((tm, D), lambda i: (i, 0))
    args = [x, gain.reshape(1, D), sh, sc]
    in_specs = [pl.BlockSpec((tm, D), lambda i: (i, 0)),
                pl.BlockSpec((1, D), lambda i: (0, 0)),
                mod_spec, mod_spec]
    out_specs = pl.BlockSpec((tm, D), lambda i: (i, 0))
    out_shape = jax.ShapeDtypeStruct((N, D), BF16)
    if with_router:
        E = router_w.shape[-1]
        args += [router_w, router_bias.astype(F32).reshape(-1, 1, E)]
        in_specs += [pl.BlockSpec((None, D, E), lambda i: (router_layer, 0, 0)),
                     pl.BlockSpec((None, 1, E), lambda i: (router_layer, 0, 0))]
        out_specs = [out_specs, pl.BlockSpec((tm, TOP_K), lambda i: (i, 0)),
                     pl.BlockSpec((tm, TOP_K), lambda i: (i, 0))]
        out_shape = [out_shape, jax.ShapeDtypeStruct((N, TOP_K), jnp.int32),
                     jax.ShapeDtypeStruct((N, TOP_K), F32)]
    return pl.pallas_call(
        functools.partial(_norm_mod_kernel, with_router=with_router),
        grid=(N // tm,),
        in_specs=in_specs,
        out_specs=out_specs,
        out_shape=out_shape,
        compiler_params=_cparams(("arbitrary",)),
        name="norm_router" if with_router else "norm_mod",
    )(*args)


def _mm_kernel(*refs, silu_in, has_bias, epilogue, tn):
    it = iter(refs)
    x_ref = next(it)
    w_ref = next(it)
    b_ref = next(it) if has_bias else None
    if epilogue == "headnorm":
        hg_ref = next(it)
    elif epilogue == "resid":
        r_ref = next(it)
        gate_ref = next(it)
    o_ref = next(it)
    wbf_ref = next(it)

    @pl.when(pl.program_id(1) == 0)
    def _():
        wbf_ref[...] = w_ref[...].astype(BF16)

    x = x_ref[...]
    if silu_in:
        x = x.astype(F32)
        x = x * _sigmoid(x)
    y = _dot(x.astype(BF16), wbf_ref[...])
    if has_bias:
        y = y + b_ref[...]
    if epilogue == "headnorm":
        gain = hg_ref[...]
        for c in range(tn // HEAD_DIM):
            sl = slice(c * HEAD_DIM, (c + 1) * HEAD_DIM)
            yc = y[:, sl]
            ms = jnp.mean(yc * yc, axis=-1, keepdims=True)
            o_ref[:, sl] = (yc * lax.rsqrt(ms + NORM_EPS) * gain).astype(o_ref.dtype)
    elif epilogue == "resid":
        o_ref[...] = (r_ref[...] + gate_ref[...] * y).astype(o_ref.dtype)
    else:
        o_ref[...] = y.astype(o_ref.dtype)


def mm(x, w, *, w_layer=None, w_col0=0, n_out=None, bias=None, silu_in=False, out_dtype=F32,
       tm=512, tn=512, epilogue=None, head_gain=None, resid=None, gate=None, rows_per_group=None,
       name="mm"):
    M, K = x.shape
    Ntot = w.shape[-1]
    N = Ntot - w_col0 if n_out is None else n_out
    tm = min(tm, M)
    tn = min(tn, N)
    assert M % tm == 0 and N % tn == 0 and w_col0 % tn == 0
    c0 = w_col0 // tn
    if w.ndim == 3:
        w_spec = pl.BlockSpec((None, K, tn), lambda j, i: (w_layer, 0, j + c0))
    else:
        w_spec = pl.BlockSpec((K, tn), lambda j, i: (0, j + c0))
    args = [x, w]
    in_specs = [pl.BlockSpec((tm, K), lambda j, i: (i, 0)), w_spec]
    if bias is not None:
        args.append(bias.reshape(1, -1))
        in_specs.append(pl.BlockSpec((1, tn), lambda j, i: (0, j + c0)))
    if epilogue == "headnorm":
        args.append(head_gain.reshape(1, HEAD_DIM))
        in_specs.append(pl.BlockSpec((1, HEAD_DIM), lambda j, i: (0, 0)))
    elif epilogue == "resid":
        args.append(resid)
        in_specs.append(pl.BlockSpec((tm, tn), lambda j, i: (i, j)))
        if rows_per_group % tm == 0:
            bpg = rows_per_group // tm
            args.append(gate.reshape(-1, 1, N))
            in_specs.append(pl.BlockSpec((None, 1, tn), lambda j, i: (i // bpg, 0, j)))
        else:
            args.append(jnp.repeat(gate, rows_per_group, axis=0))
            in_specs.append(pl.BlockSpec((tm, tn), lambda j, i: (i, j)))
    kern = functools.partial(_mm_kernel, silu_in=silu_in, has_bias=bias is not None,
                             epilogue=epilogue, tn=tn)
    return pl.pallas_call(
        kern,
        grid=(N // tn, M // tm),
        in_specs=in_specs,
        out_specs=pl.BlockSpec((tm, tn), lambda j, i: (i, j)),
        out_shape=jax.ShapeDtypeStruct((M, N), out_dtype),
        scratch_shapes=[pltpu.VMEM((K, tn), BF16)],
        compiler_params=_cparams(("arbitrary", "arbitrary")),
        name=name,
    )(*args)


def _hgrn_constants(C):
    r = np.arange(C)
    mats = [(r[None, :] <= r[:, None]).astype(np.float32)]
    masks = [np.eye(C, dtype=np.float32)]
    m = C // 2
    while m >= 1:
        base = (r // (2 * m)) * (2 * m)
        pivot = base + m - 1
        upper = r > pivot
        j = r[None, :]
        mat = np.where(upper[:, None], (j > pivot[:, None]) & (j <= r[:, None]),
                       (j > r[:, None]) & (j <= pivot[:, None]))
        mats.append(mat.astype(np.float32))
        same = base[:, None] == base[None, :]
        masks.append((same & upper[:, None] & (~upper)[None, :]).astype(np.float32))
        m //= 2
    return np.concatenate(mats, axis=0), np.stack(masks, axis=0)


def _hgrn_kernel(q_ref, fu_ref, i_ref, g_ref, lb_ref, gain_ref, s0_ref, mat_ref, mask_ref,
                 o_ref, sout_ref, s_ref, *, C, n_chunks, n_levels, t_valid):
    t = pl.program_id(2)

    @pl.when(t == 0)
    def _():
        s_ref[...] = s0_ref[...].T

    lb = lb_ref[...]
    gain = gain_ref[...]
    staged = []
    for c in range(n_chunks):
        rows = slice(c * C, (c + 1) * C)
        q = q_ref[rows, :]
        f = lb + (1.0 - lb) * _sigmoid(fu_ref[rows, :])
        logf = jnp.log(f)
        kk = 1.0 - f
        if t_valid is not None:
            valid = (lax.broadcasted_iota(jnp.int32, (C, HEAD_DIM), 0) + c * C) < t_valid
            logf = jnp.where(valid, logf, 0.0)
            kk = jnp.where(valid, kk, 0.0)
        hi = logf.astype(BF16)
        lo = (logf - hi.astype(F32)).astype(BF16)
        e = _dot(mat_ref[...], hi) + _dot(mat_ref[...], lo)
        b = e[0:C, :]
        scores = mask_ref[0] * _dot_nt(q.astype(BF16), kk.astype(BF16))
        for l in range(1, n_levels + 1):
            x = jnp.exp(e[l * C:(l + 1) * C, :])
            scores = scores + mask_ref[l] * _dot_nt((q * x).astype(BF16), (kk * x).astype(BF16))
        b_end = b[C - 1:C, :]
        staged.append(((q * jnp.exp(b)).astype(BF16), scores.astype(BF16), i_ref[rows, :].astype(BF16),
                       (kk * jnp.exp(b_end - b)).astype(BF16), jnp.exp(b_end)))

    St = s_ref[...]
    for c, (q_dec, scores, vb, khat, decay) in enumerate(staged):
        rows = slice(c * C, (c + 1) * C)
        o = _dot_nt(q_dec, St.astype(BF16)) + _dot(scores, vb)
        St = decay * St + _dot_tn(vb, khat)
        ms = jnp.mean(o * o, axis=-1, keepdims=True)
        gt = g_ref[rows, :]
        o_ref[rows, :] = (o * lax.rsqrt(ms + NORM_EPS) * gain * (gt * _sigmoid(gt))).astype(o_ref.dtype)
    s_ref[...] = St

    @pl.when(t == pl.num_programs(2) - 1)
    def _():
        sout_ref[...] = s_ref[...].T


def hgrn2(qfig, lb, gain, s0, *, B, T, C, tb, t_valid=None):
    H = s0.shape[1]
    HD = H * HEAD_DIM
    x3 = qfig.reshape(B, T, 4 * HD)
    mats, masks = _hgrn_constants(C)
    n_levels = masks.shape[0] - 1
    kern = functools.partial(_hgrn_kernel, C=C, n_chunks=tb // C, n_levels=n_levels, t_valid=t_valid)

    def col(off):
        return pl.BlockSpec((None, tb, HEAD_DIM), lambda b, h, t: (b, t, h + off * H))

    head_vec = pl.BlockSpec((1, HEAD_DIM), lambda b, h, t: (0, h))
    o, s_out = pl.pallas_call(
        kern,
        grid=(B, H, T // tb),
        in_specs=[col(0), col(1), col(2), col(3), head_vec, head_vec,
                  pl.BlockSpec((None, None, HEAD_DIM, HEAD_DIM), lambda b, h, t: (b, h, 0, 0)),
                  pl.BlockSpec(mats.shape, lambda b, h, t: (0, 0)),
                  pl.BlockSpec(masks.shape, lambda b, h, t: (0, 0, 0))],
        out_specs=[pl.BlockSpec((None, tb, HEAD_DIM), lambda b, h, t: (b, t, h)),
                   pl.BlockSpec((None, None, HEAD_DIM, HEAD_DIM), lambda b, h, t: (b, h, 0, 0))],
        out_shape=[jax.ShapeDtypeStruct((B, T, HD), BF16),
                   jax.ShapeDtypeStruct(s0.shape, F32)],
        scratch_shapes=[pltpu.VMEM((HEAD_DIM, HEAD_DIM), F32)],
        compiler_params=_cparams(("arbitrary", "arbitrary", "arbitrary")),
        name="hgrn2",
    )(x3, x3, x3, x3, lb.reshape(1, HD), gain.reshape(1, HD), s0,
      jnp.asarray(mats, BF16), jnp.asarray(masks))
    return o.reshape(B * T, HD), s_out


def _log_sig_pair(z):
    lk = -jnp.maximum(z, 0.0) - jnp.log1p(jnp.exp(-jnp.abs(z)))
    return z + lk, lk


def _suffix_sum(lk, u):
    hi = lk.astype(BF16)
    lo = (lk - hi.astype(F32)).astype(BF16)
    return _dot(hi, u) + _dot(lo, u)


def _sb_block(z, visible, u, rsum):
    ls, lk = _log_sig_pair(z)
    if visible is not None:
        lk = jnp.where(visible, lk, 0.0)
    la = _suffix_sum(lk, u) + rsum
    a = jnp.exp(ls + la)
    if visible is not None:
        a = jnp.where(visible, a, 0.0)
    return a, rsum + jnp.sum(lk, axis=-1, keepdims=True)


def _sb_prompt_kernel(bias_ref, q_ref, k_ref, v_ref, u_ref, o_ref, *, tq, tk, scale):
    h = pl.program_id(1)
    qi = pl.program_id(2)
    bias = bias_ref[h]
    q = q_ref[...].astype(BF16)
    u = u_ref[...]
    nd = tq // tk

    def block(kb, visible, rsum, acc):
        start = pl.multiple_of(kb * tk, tk)
        k = k_ref[pl.ds(start, tk), :].astype(BF16)
        v = v_ref[pl.ds(start, tk), :].astype(BF16)
        z = _dot_nt(q, k) * scale + bias
        a, rsum = _sb_block(z, visible, u, rsum)
        return rsum, acc + _dot(a.astype(BF16), v)

    row = lax.broadcasted_iota(jnp.int32, (tq, tk), 0)
    col = lax.broadcasted_iota(jnp.int32, (tq, tk), 1)
    carry = (jnp.zeros((tq, 1), F32), jnp.zeros((tq, HEAD_DIM), F32))
    for d in reversed(range(nd)):
        carry = block(qi * nd + d, col + d * tk < row, *carry)

    def body(it, carry):
        for d in range(nd):
            carry = block((qi - it) * nd - 1 - d, None, *carry)
        return carry

    rsum, acc = lax.fori_loop(0, qi, body, carry)
    o_ref[...] = acc.astype(o_ref.dtype)


def sb_attention_prompt(q, k, v, bias, *, B, T, tq=512, tk=256):
    N, HD = q.shape
    H = HD // HEAD_DIM
    tq = min(tq, T)
    tk = min(tk, tq)
    nq = T // tq
    u = jnp.asarray(np.triu(np.ones((tk, tk), np.float32), 0).T - np.eye(tk, dtype=np.float32), BF16)
    kern = functools.partial(_sb_prompt_kernel, tq=tq, tk=tk, scale=HEAD_DIM ** -0.5)
    return pl.pallas_call(
        kern,
        grid_spec=pltpu.PrefetchScalarGridSpec(
            num_scalar_prefetch=1,
            grid=(B, H, nq),
            in_specs=[pl.BlockSpec((tq, HEAD_DIM), lambda b, h, i, bias: (b * nq + i, h)),
                      pl.BlockSpec((T, HEAD_DIM), lambda b, h, i, bias: (b, h)),
                      pl.BlockSpec((T, HEAD_DIM), lambda b, h, i, bias: (b, h)),
                      pl.BlockSpec((tk, tk), lambda b, h, i, bias: (0, 0))],
            out_specs=pl.BlockSpec((tq, HEAD_DIM), lambda b, h, i, bias: (b * nq + i, h)),
        ),
        out_shape=jax.ShapeDtypeStruct((N, HD), BF16),
        compiler_params=_cparams(("arbitrary", "arbitrary", "arbitrary")),
        name="sb_attn_prompt",
    )(bias.astype(F32), q, k, v, u)


def _sb_sample_kernel(pt_ref, q_ref, bias_ref, u_ref, new_k, new_v, *rest, scale, G):
    pages = rest[:2 * G]
    o_ref, rsum_ref, acc_ref = rest[2 * G:]
    j = pl.program_id(1)
    H, tp, _ = q_ref.shape
    P = new_k.shape[0] // H
    R = H * tp

    @pl.when(j == 0)
    def _():
        rsum_ref[...] = jnp.zeros_like(rsum_ref)
        acc_ref[...] = jnp.zeros_like(acc_ref)

    def head_rows(ref, h):
        return ref[pl.ds(h, P, stride=H), :].astype(BF16)

    def block(k_ref, v_ref, visible, rsum, acc):
        z = jnp.concatenate([_dot_nt(q_ref[h], head_rows(k_ref, h)) for h in range(H)], axis=0)
        z = z * scale + bias_ref[...]
        a, rsum = _sb_block(z, visible, u_ref[...], rsum)
        acc = [acc[h] + _dot(a[h * tp:(h + 1) * tp, :].astype(BF16), head_rows(v_ref, h)) for h in range(H)]
        return rsum, acc

    def run(blocks):
        rsum = rsum_ref[...]
        acc = [acc_ref[h] for h in range(H)]
        for k_ref, v_ref, visible in blocks:
            rsum, acc = block(k_ref, v_ref, visible, rsum, acc)
        rsum_ref[...] = rsum
        for h in range(H):
            acc_ref[h] = acc[h]

    @pl.when(j == 0)
    def _():
        t_of_row = lax.broadcasted_iota(jnp.int32, (R, P), 0) % tp
        run([(new_k, new_v, lax.broadcasted_iota(jnp.int32, (R, P), 1) < t_of_row)])

    @pl.when(j > 0)
    def _():
        run([(pages[2 * g], pages[2 * g + 1], None) for g in range(G)])

    @pl.when(j == pl.num_programs(1) - 1)
    def _():
        o_ref[...] = acc_ref[...]


def sb_attention_sample(q, k_new, v_new, cache_k, cache_v, page_table, bias, *, B, T):
    HD = q.shape[1]
    H = HD // HEAD_DIM
    n_pages = page_table.shape[1]
    P = cache_k.shape[1]
    tp = SAMPLE_Q_ROWS
    R = H * tp
    qh = jnp.pad(q.reshape(B, T, H, HEAD_DIM).transpose(0, 2, 1, 3), ((0, 0), (0, 0), (0, tp - T), (0, 0)))
    bias_rows = jnp.repeat(bias.astype(F32), tp).reshape(R, 1)
    u = jnp.asarray(np.triu(np.ones((P, P), np.float32), 0).T - np.eye(P, dtype=np.float32), BF16)
    G = SAMPLE_PAGES_PER_STEP
    assert n_pages % G == 0
    page_view = lambda a: a.reshape(a.shape[0], P * H, HEAD_DIM)
    new_view = lambda a: page_view(jnp.pad(a.reshape(B, T, H, HEAD_DIM), ((0, 0), (0, P - T), (0, 0), (0, 0))))
    page = (None, P * H, HEAD_DIM)

    def page_spec(g):
        return pl.BlockSpec(page, lambda b, j, pt: (pt[b, n_pages - 1 - (jnp.maximum(j, 1) - 1) * G - g], 0, 0))

    in_specs = [pl.BlockSpec((None, H, tp, HEAD_DIM), lambda b, j, pt: (b, 0, 0, 0)),
                pl.BlockSpec((R, 1), lambda b, j, pt: (0, 0)),
                pl.BlockSpec((P, P), lambda b, j, pt: (0, 0)),
                pl.BlockSpec(page, lambda b, j, pt: (b, 0, 0)),
                pl.BlockSpec(page, lambda b, j, pt: (b, 0, 0))]
    args = [page_table, qh.astype(BF16), bias_rows, u, new_view(k_new), new_view(v_new)]
    ck, cv = page_view(cache_k), page_view(cache_v)
    for g in range(G):
        in_specs += [page_spec(g), page_spec(g)]
        args += [ck, cv]
    o = pl.pallas_call(
        functools.partial(_sb_sample_kernel, scale=HEAD_DIM ** -0.5, G=G),
        grid_spec=pltpu.PrefetchScalarGridSpec(
            num_scalar_prefetch=1,
            grid=(B, n_pages // G + 1),
            in_specs=in_specs,
            out_specs=pl.BlockSpec((None, H, tp, HEAD_DIM), lambda b, j, pt: (b, 0, 0, 0)),
            scratch_shapes=[pltpu.VMEM((R, 1), F32), pltpu.VMEM((H, tp, HEAD_DIM), F32)],
        ),
        out_shape=jax.ShapeDtypeStruct((B, H, tp, HEAD_DIM), F32),
        compiler_params=_cparams(("arbitrary", "arbitrary")),
        name="sb_attn_sample",
    )(*args)
    return o[:, :, :T].transpose(0, 2, 1, 3).reshape(B * T, HD).astype(BF16)


def _experts_kernel(be_ref, nb_ref, x_ref, wg_ref, wu_ref, wd_ref, o_ref, wgb_ref, wub_ref, wdb_ref):
    i = pl.program_id(0)
    prev = be_ref[jnp.maximum(i - 1, 0)]
    fresh = jnp.logical_or(i == 0, be_ref[i] != prev)

    @pl.when(fresh)
    def _():
        wgb_ref[...] = wg_ref[...].astype(BF16)
        wub_ref[...] = wu_ref[...].astype(BF16)
        wdb_ref[...] = wd_ref[...].astype(BF16)

    @pl.when(i < nb_ref[0])
    def _():
        x = x_ref[...]
        g = _dot(x, wgb_ref[...])
        up = _dot(x, wub_ref[...])
        a = (g * _sigmoid(g) * up).astype(BF16)
        o_ref[...] = _dot(a, wdb_ref[...]).astype(o_ref.dtype)

    @pl.when(i >= nb_ref[0])
    def _():
        o_ref[...] = jnp.zeros_like(o_ref)


def experts(x, block_e, n_used, wg, wu, wd, layer, tm, out_dtype, name):
    NR, D = x.shape
    Fd = wg.shape[-1]
    n_blocks = NR // tm
    return pl.pallas_call(
        _experts_kernel,
        grid_spec=pltpu.PrefetchScalarGridSpec(
            num_scalar_prefetch=2,
            grid=(n_blocks,),
            in_specs=[pl.BlockSpec((tm, D), lambda i, be, nb: (i, 0)),
                      pl.BlockSpec((None, None, D, Fd), lambda i, be, nb: (layer, be[i], 0, 0)),
                      pl.BlockSpec((None, None, D, Fd), lambda i, be, nb: (layer, be[i], 0, 0)),
                      pl.BlockSpec((None, None, Fd, D), lambda i, be, nb: (layer, be[i], 0, 0))],
            out_specs=pl.BlockSpec((tm, D), lambda i, be, nb: (i, 0)),
            scratch_shapes=[pltpu.VMEM((D, Fd), BF16), pltpu.VMEM((D, Fd), BF16),
                            pltpu.VMEM((Fd, D), BF16)],
        ),
        out_shape=jax.ShapeDtypeStruct((NR, D), out_dtype),
        compiler_params=_cparams(("arbitrary",)),
        name=name,
    )(block_e, n_used, x, wg, wu, wd)


def _combine_kernel(x_ref, gate_ref, sh_ref, wt_ref, g_ref, o_ref):
    wts = wt_ref[...]
    y = sh_ref[...]
    for k in range(TOP_K):
        y = y + wts[:, k:k + 1] * g_ref[k].astype(F32)
    o_ref[...] = x_ref[...] + gate_ref[...] * y


def combine(x, gate, shared, wts, gathered, rows_per_group, tm):
    N, D = x.shape
    tm = min(tm, N)
    if rows_per_group % tm == 0:
        bpg = rows_per_group // tm
        gate = gate.reshape(-1, 1, D)
        gate_spec = pl.BlockSpec((None, 1, D), lambda i: (i // bpg, 0, 0))
    else:
        gate = jnp.repeat(gate, rows_per_group, axis=0)
        gate_spec = pl.BlockSpec((tm, D), lambda i: (i, 0))
    row = pl.BlockSpec((tm, D), lambda i: (i, 0))
    return pl.pallas_call(
        _combine_kernel,
        grid=(N // tm,),
        in_specs=[row, gate_spec, row, pl.BlockSpec((tm, TOP_K), lambda i: (i, 0)),
                  pl.BlockSpec((TOP_K, tm, D), lambda i: (0, i, 0))],
        out_specs=row,
        out_shape=jax.ShapeDtypeStruct((N, D), F32),
        compiler_params=_cparams(("arbitrary",)),
        name="moe_combine",
    )(x, gate, shared, wts, gathered)


def moe_ffn(x, gate, rows_per_group, h, idx, wts, layer, exp_w_gate, exp_w_up, exp_w_down,
            sh_w_gate, sh_w_up, sh_w_down, tm_routed, tm_shared):
    N, D = h.shape

    tm = tm_routed
    NK = N * TOP_K
    n_blocks = -(-(NK + N_EXPERTS * (tm - 1)) // tm)
    member = jnp.sum(idx[:, :, None] == jnp.arange(N_EXPERTS)[None, None, :], axis=1).astype(jnp.int32)
    incl = jnp.cumsum(member, axis=0)
    counts = incl[-1]
    rank = jnp.take_along_axis(incl - member, idx, axis=1)
    padded = (counts + tm - 1) // tm * tm
    pend = jnp.cumsum(padded)
    pstart = pend - padded
    dest = pstart[idx] + rank
    tok = jnp.broadcast_to(jnp.arange(N, dtype=jnp.int32)[:, None], (N, TOP_K))
    rows = jnp.zeros((n_blocks * tm,), jnp.int32).at[dest.reshape(-1)].set(tok.reshape(-1))
    block_start = jnp.arange(n_blocks, dtype=jnp.int32) * tm
    block_e = jnp.minimum(jnp.sum(pend[None, :] <= block_start[:, None], axis=1), N_EXPERTS - 1).astype(jnp.int32)
    n_used = (pend[-1] // tm).astype(jnp.int32).reshape(1)
    xg = jnp.pad(h, ((0, n_blocks * tm - N), (0, 0)))[rows]
    yb = experts(xg, block_e, n_used, exp_w_gate, exp_w_up, exp_w_down, layer, tm, BF16, "experts_routed")
    gathered = yb[dest.T.reshape(-1)].reshape(TOP_K, N, D)

    nsb = N // tm_shared
    shared = experts(h, jnp.zeros((nsb,), jnp.int32), jnp.full((1,), nsb, jnp.int32),
                     sh_w_gate[:, None], sh_w_up[:, None], sh_w_down[:, None], layer, tm_shared, F32,
                     "experts_shared")
    return combine(x, gate, shared, wts, gathered, rows_per_group, COMBINE_ROWS)


def trunk(x, mods, kv_mod, s0, past, p, cfg):
    B, T, D = x.shape
    N = B * T
    x = x.reshape(N, D)
    tm = cfg["tm"]
    HD = p["hg_w_out"].shape[1]
    lower = jnp.cumsum(jax.nn.softmax(p["hg_lb"].astype(F32), axis=0), axis=0)

    def split6(m):
        return [m[:, i * D:(i + 1) * D] for i in range(6)]

    sh_m, sc_m, g_m, sh_f, sc_f, g_f = split6(mods[0])
    h = norm_mod(x, p["norm_mix"][0], sh_m, sc_m, T, tm)
    qfig = mm(h, p["hg_w_in"], w_layer=0, tm=tm, name="hg_in")
    Tp = cfg["hg_t_pad"]
    if Tp != T:
        qfig = jnp.pad(qfig.reshape(B, T, -1), ((0, 0), (0, Tp - T), (0, 0))).reshape(B * Tp, -1)
    o, s_new = hgrn2(qfig, lower[0], p["hg_norm"][0], s0, B=B, T=Tp, C=cfg["hg_chunk"], tb=cfg["hg_tb"],
                     t_valid=None if Tp == T else T)
    if Tp != T:
        o = o.reshape(B, Tp, HD)[:, :T].reshape(N, HD)
    x = mm(o, p["hg_w_out"], w_layer=0, tm=tm, epilogue="resid", resid=x, gate=g_m, rows_per_group=T,
           name="hg_out")
    h, idx, wts = norm_mod(x, p["norm_ffn"][0], sh_f, sc_f, T, tm, router_w=p["router_w"],
                           router_bias=p["router_bias"], router_layer=0)
    x = moe_ffn(x, g_f, T, h, idx, wts, 0, p["exp_w_gate"], p["exp_w_up"], p["exp_w_down"],
                p["sh_w_gate"], p["sh_w_up"], p["sh_w_down"], cfg["tm_routed"], cfg["tm_shared"])

    sh_kv, sc_kv = kv_mod[:, :D], kv_mod[:, D:]
    hk = norm_mod(x, p["kv_norm"], sh_kv, sc_kv, T, tm)
    k_new = mm(hk, p["w_kv"], n_out=HD, tm=tm, epilogue="headnorm", head_gain=p["k_norm"], name="k_proj")
    v_new = mm(hk, p["w_kv"], w_col0=HD, n_out=HD, tm=tm, name="v_proj")

    sh_m, sc_m, g_m, sh_f, sc_f, g_f = split6(mods[1])
    h = norm_mod(x, p["norm_mix"][1], sh_m, sc_m, T, tm)
    q = mm(h, p["sb_w_q"], w_layer=0, tm=tm, epilogue="headnorm", head_gain=p["q_norm"][0], name="q_proj")
    bias = p["sb_logit_bias"][0]
    if past is None:
        o = sb_attention_prompt(q, k_new, v_new, bias, B=B, T=T)
    else:
        o = sb_attention_sample(q, k_new, v_new, past[0], past[1], past[2], bias, B=B, T=T)
    x = mm(o, p["sb_w_out"], w_layer=0, tm=tm, epilogue="resid", resid=x, gate=g_m, rows_per_group=T,
           name="sb_out")
    h, idx, wts = norm_mod(x, p["norm_ffn"][1], sh_f, sc_f, T, tm, router_w=p["router_w"],
                           router_bias=p["router_bias"], router_layer=1)
    x = moe_ffn(x, g_f, T, h, idx, wts, 1, p["exp_w_gate"], p["exp_w_up"], p["exp_w_down"],
                p["sh_w_gate"], p["sh_w_up"], p["sh_w_down"], cfg["tm_routed"], cfg["tm_shared"])
    H = HD // HEAD_DIM
    return (x.reshape(B, T, D), s_new[None], k_new.reshape(B, T, H, HEAD_DIM), v_new.reshape(B, T, H, HEAD_DIM))


PROMPT_CFG = dict(tm=512, hg_chunk=HG_CHUNK, hg_tb=256, hg_t_pad=None, tm_routed=512, tm_shared=512)
SAMPLE_CFG = dict(tm=32, hg_chunk=HG_CHUNK, hg_tb=HG_CHUNK, hg_t_pad=HG_CHUNK, tm_routed=16, tm_shared=32)
MOD_ROWS = 16


def kernel(x_prompt, x_sample, c_prompt, c_sample, state_hgrn, cache_k, cache_v, page_table, ada_w, ada_b,
           norm_mix, norm_ffn, hg_w_in, hg_lb, hg_norm, hg_w_out, kv_ada_w, kv_ada_b, kv_norm, w_kv, k_norm,
           sb_w_q, q_norm, sb_logit_bias, sb_w_out, router_w, router_bias, exp_w_gate, exp_w_up, exp_w_down,
           sh_w_gate, sh_w_up, sh_w_down):
    p = dict(ada_w=ada_w, ada_b=ada_b, norm_mix=norm_mix, norm_ffn=norm_ffn, hg_w_in=hg_w_in, hg_lb=hg_lb,
             hg_norm=hg_norm, hg_w_out=hg_w_out, kv_ada_w=kv_ada_w, kv_ada_b=kv_ada_b, kv_norm=kv_norm,
             w_kv=w_kv, k_norm=k_norm, sb_w_q=sb_w_q, q_norm=q_norm, sb_logit_bias=sb_logit_bias,
             sb_w_out=sb_w_out, router_w=router_w, router_bias=router_bias, exp_w_gate=exp_w_gate,
             exp_w_up=exp_w_up, exp_w_down=exp_w_down, sh_w_gate=sh_w_gate, sh_w_up=sh_w_up,
             sh_w_down=sh_w_down)
    Bp, Tp, _ = x_prompt.shape
    Bs = x_sample.shape[0]
    H = hg_w_out.shape[1] // HEAD_DIM
    c_all = jnp.concatenate([c_prompt, c_sample], axis=0)
    c_all = jnp.pad(c_all, ((0, MOD_ROWS - c_all.shape[0]), (0, 0)))
    mods = [mm(c_all, ada_w, w_layer=l, bias=ada_b[l], silu_in=True, tn=1024, name="ada") for l in range(2)]
    kv_mod = mm(c_all, kv_ada_w, bias=kv_ada_b, silu_in=True, tn=1024, name="ada_kv")
    rows_p = slice(0, Bp)
    rows_s = slice(Bp, Bp + Bs)
    cfg_p = dict(PROMPT_CFG, hg_t_pad=Tp)
    s0_prompt = jnp.zeros((Bp, H, HEAD_DIM, HEAD_DIM), state_hgrn.dtype)
    y_p, s_p, k_p, v_p = trunk(x_prompt, [m[rows_p] for m in mods], kv_mod[rows_p], s0_prompt, None, p, cfg_p)
    y_s, s_s, k_s, v_s = trunk(x_sample, [m[rows_s] for m in mods], kv_mod[rows_s], state_hgrn[0],
                               (cache_k, cache_v, page_table), p, SAMPLE_CFG)
    return (y_p, y_s, s_p, s_s, k_p, v_p, k_s, v_s)
```

```python
import functools
import math

import numpy as np
import jax
import jax.numpy as jnp
from jax import lax
from jax.experimental import pallas as pl
from jax.experimental.pallas import tpu as pltpu

F32 = jnp.float32
BF16 = jnp.bfloat16

LANES = 128
SUBLANES = 8
SAMPLE_PAGES_PER_STEP = 4
VMEM_LIMIT_BYTES = 52 * 1024 * 1024
NORM_EPS = 1e-6
HEAD_DIM = 128
N_EXPERTS = 64
TOP_K = 8
ROUTED_SCALE = 2.5
HG_CHUNK = 64
COMBINE_ROWS = 256
SAMPLE_Q_ROWS = 16


def _cparams(sem):
    return pltpu.CompilerParams(dimension_semantics=sem, vmem_limit_bytes=VMEM_LIMIT_BYTES)


def _sigmoid(x):
    return 1.0 / (1.0 + jnp.exp(-x))


def _dot(a, b):
    return jnp.dot(a, b, preferred_element_type=F32)


def _dot_nt(a, b):
    return lax.dot_general(a, b, (((1,), (1,)), ((), ())), preferred_element_type=F32)


def _dot_tn(a, b):
    return lax.dot_general(a, b, (((0,), (0,)), ((), ())), preferred_element_type=F32)


def _norm_mod_kernel(x_ref, g_ref, sh_ref, sc_ref, *rest, with_router):
    x = x_ref[...]
    ms = jnp.mean(x * x, axis=-1, keepdims=True)
    y = x * lax.rsqrt(ms + NORM_EPS) * g_ref[...]
    h = y * (1.0 + sc_ref[...]) + sh_ref[...]
    if with_router:
        rw_ref, rb_ref, o_ref, idx_ref, wt_ref = rest
        logits = jnp.dot(h, rw_ref[...], preferred_element_type=F32, precision=lax.Precision.HIGHEST)
        scores = _sigmoid(logits)
        sel = scores + rb_ref[...]
        tm, E = sel.shape
        lane = lax.broadcasted_iota(jnp.int32, (tm, E), 1)
        slot = lax.broadcasted_iota(jnp.int32, (tm, TOP_K), 1)
        idx = jnp.zeros((tm, TOP_K), jnp.int32)
        wts = jnp.zeros((tm, TOP_K), F32)
        for k in range(TOP_K):
            best = jnp.max(sel, axis=-1, keepdims=True)
            pick = jnp.min(jnp.where(sel == best, lane, E), axis=-1, keepdims=True)
            hit = lane == pick
            w_k = jnp.sum(jnp.where(hit, scores, 0.0), axis=-1, keepdims=True)
            sel = jnp.where(hit, -jnp.inf, sel)
            idx = jnp.where(slot == k, pick, idx)
            wts = jnp.where(slot == k, w_k, wts)
        idx_ref[...] = idx
        wt_ref[...] = wts / jnp.sum(wts, axis=-1, keepdims=True) * ROUTED_SCALE
    else:
        o_ref, = rest
    o_ref[...] = h.astype(o_ref.dtype)


def norm_mod(x, gain, shift, scale, rows_per_group, tm, router_w=None, router_bias=None, router_layer=None):
    N, D = x.shape
    tm = min(tm, N)
    with_router = router_w is not None
    if rows_per_group % tm == 0:
        bpg = rows_per_group // tm
        sh = shift.reshape(-1, 1, D)
        sc = scale.reshape(-1, 1, D)
        mod_spec = pl.BlockSpec((None, 1, D), lambda i: (i // bpg, 0, 0))
    else:
        sh = jnp.repeat(shift, rows_per_group, axis=0)
        sc = jnp.repeat(scale, rows_per_group, axis=0)
        mod_spec = pl.BlockSpec((tm, D), lambda i: (i, 0))
    args = [x, gain.reshape(1, D), sh, sc]
    in_specs = [pl.BlockSpec((tm, D), lambda i: (i, 0)),
                pl.BlockSpec((1, D), lambda i: (0, 0)),
                mod_spec, mod_spec]
    out_specs = pl.BlockSpec((tm, D), lambda i: (i, 0))
    out_shape = jax.ShapeDtypeStruct((N, D), BF16)
    if with_router:
        E = router_w.shape[-1]
        args += [router_w, router_bias.astype(F32).reshape(-1, 1, E)]
        in_specs += [pl.BlockSpec((None, D, E), lambda i: (router_layer, 0, 0)),
                     pl.BlockSpec((None, 1, E), lambda i: (router_layer, 0, 0))]
        out_specs = [out_specs, pl.BlockSpec((tm, TOP_K), lambda i: (i, 0)),
                     pl.BlockSpec((tm, TOP_K), lambda i: (i, 0))]
        out_shape = [out_shape, jax.ShapeDtypeStruct((N, TOP_K), jnp.int32),
                     jax.ShapeDtypeStruct((N, TOP_K), F32)]
    return pl.pallas_call(
        functools.partial(_norm_mod_kernel, with_router=with_router),
        grid=(N // tm,),
        in_specs=in_specs,
        out_specs=out_specs,
        out_shape=out_shape,
        compiler_params=_cparams(("arbitrary",)),
        name="norm_router" if with_router else "norm_mod",
    )(*args)


def _mm_kernel(*refs, silu_in, has_bias, epilogue, tn):
    it = iter(refs)
    x_ref = next(it)
    w_ref = next(it)
    b_ref = next(it) if has_bias else None
    if epilogue == "headnorm":
        hg_ref = next(it)
    elif epilogue == "resid":
        r_ref = next(it)
        gate_ref = next(it)
    o_ref = next(it)
    wbf_ref = next(it)

    @pl.when(pl.program_id(1) == 0)
    def _():
        wbf_ref[...] = w_ref[...].astype(BF16)

    x = x_ref[...]
    if silu_in:
        x = x.astype(F32)
        x = x * _sigmoid(x)
    y = _dot(x.astype(BF16), wbf_ref[...])
    if has_bias:
        y = y + b_ref[...]
    if epilogue == "headnorm":
        gain = hg_ref[...]
        for c in range(tn // HEAD_DIM):
            sl = slice(c * HEAD_DIM, (c + 1) * HEAD_DIM)
            yc = y[:, sl]
            ms = jnp.mean(yc * yc, axis=-1, keepdims=True)
            o_ref[:, sl] = (yc * lax.rsqrt(ms + NORM_EPS) * gain).astype(o_ref.dtype)
    elif epilogue == "resid":
        o_ref[...] = (r_ref[...] + gate_ref[...] * y).astype(o_ref.dtype)
    else:
        o_ref[...] = y.astype(o_ref.dtype)


def mm(x, w, *, w_layer=None, w_col0=0, n_out=None, bias=None, silu_in=False, out_dtype=F32,
       tm=512, tn=512, epilogue=None, head_gain=None, resid=None, gate=None, rows_per_group=None,
       name="mm"):
    M, K = x.shape
    Ntot = w.shape[-1]
    N = Ntot - w_col0 if n_out is None else n_out
    tm = min(tm, M)
    tn = min(tn, N)
    assert M % tm == 0 and N % tn == 0 and w_col0 % tn == 0
    c0 = w_col0 // tn
    if w.ndim == 3:
        w_spec = pl.BlockSpec((None, K, tn), lambda j, i: (w_layer, 0, j + c0))
    else:
        w_spec = pl.BlockSpec((K, tn), lambda j, i: (0, j + c0))
    args = [x, w]
    in_specs = [pl.BlockSpec((tm, K), lambda j, i: (i, 0)), w_spec]
    if bias is not None:
        args.append(bias.reshape(1, -1))
        in_specs.append(pl.BlockSpec((1, tn), lambda j, i: (0, j + c0)))
    if epilogue == "headnorm":
        args.append(head_gain.reshape(1, HEAD_DIM))
        in_specs.append(pl.BlockSpec((1, HEAD_DIM), lambda j, i: (0, 0)))
    elif epilogue == "resid":
        args.append(resid)
        in_specs.append(pl.BlockSpec((tm, tn), lambda j, i: (i, j)))
        if rows_per_group % tm == 0:
            bpg = rows_per_group // tm
            args.append(gate.reshape(-1, 1, N))
            in_specs.append(pl.BlockSpec((None, 1, tn), lambda j, i: (i // bpg, 0, j)))
        else:
            args.append(jnp.repeat(gate, rows_per_group, axis=0))
            in_specs.append(pl.BlockSpec((tm, tn), lambda j, i: (i, j)))
    kern = functools.partial(_mm_kernel, silu_in=silu_in, has_bias=bias is not None,
                             epilogue=epilogue, tn=tn)
    return pl.pallas_call(
        kern,
        grid=(N // tn, M // tm),
        in_specs=in_specs,
        out_specs=pl.BlockSpec((tm, tn), lambda j, i: (i, j)),
        out_shape=jax.ShapeDtypeStruct((M, N), out_dtype),
        scratch_shapes=[pltpu.VMEM((K, tn), BF16)],
        compiler_params=_cparams(("arbitrary", "arbitrary")),
        name=name,
    )(*args)


def _hgrn_constants(C):
    r = np.arange(C)
    mats = [(r[None, :] <= r[:, None]).astype(np.float32)]
    masks = [np.eye(C, dtype=np.float32)]
    m = C // 2
    while m >= 1:
        base = (r // (2 * m)) * (2 * m)
        pivot = base + m - 1
        upper = r > pivot
        j = r[None, :]
        mat = np.where(upper[:, None], (j > pivot[:, None]) & (j <= r[:, None]),
                       (j > r[:, None]) & (j <= pivot[:, None]))
        mats.append(mat.astype(np.float32))
        same = base[:, None] == base[None, :]
        masks.append((same & upper[:, None] & (~upper)[None, :]).astype(np.float32))
        m //= 2
    return np.concatenate(mats, axis=0), np.stack(masks, axis=0)


def _hgrn_kernel(q_ref, fu_ref, i_ref, g_ref, lb_ref, gain_ref, s0_ref, mat_ref, mask_ref,
                 o_ref, sout_ref, s_ref, *, C, n_chunks, n_levels, t_valid):
    t = pl.program_id(2)

    @pl.when(t == 0)
    def _():
        s_ref[...] = s0_ref[...].T

    lb = lb_ref[...]
    gain = gain_ref[...]
    staged = []
    for c in range(n_chunks):
        rows = slice(c * C, (c + 1) * C)
        q = q_ref[rows, :]
        f = lb + (1.0 - lb) * _sigmoid(fu_ref[rows, :])
        logf = jnp.log(f)
        kk = 1.0 - f
        if t_valid is not None:
            valid = (lax.broadcasted_iota(jnp.int32, (C, HEAD_DIM), 0) + c * C) < t_valid
            logf = jnp.where(valid, logf, 0.0)
            kk = jnp.where(valid, kk, 0.0)
        hi = logf.astype(BF16)
        lo = (logf - hi.astype(F32)).astype(BF16)
        e = _dot(mat_ref[...], hi) + _dot(mat_ref[...], lo)
        b = e[0:C, :]
        scores = mask_ref[0] * _dot_nt(q.astype(BF16), kk.astype(BF16))
        for l in range(1, n_levels + 1):
            x = jnp.exp(e[l * C:(l + 1) * C, :])
            scores = scores + mask_ref[l] * _dot_nt((q * x).astype(BF16), (kk * x).astype(BF16))
        b_end = b[C - 1:C, :]
        staged.append(((q * jnp.exp(b)).astype(BF16), scores.astype(BF16), i_ref[rows, :].astype(BF16),
                       (kk * jnp.exp(b_end - b)).astype(BF16), jnp.exp(b_end)))

    St = s_ref[...]
    for c, (q_dec, scores, vb, khat, decay) in enumerate(staged):
        rows = slice(c * C, (c + 1) * C)
        o = _dot_nt(q_dec, St.astype(BF16)) + _dot(scores, vb)
        St = decay * St + _dot_tn(vb, khat)
        ms = jnp.mean(o * o, axis=-1, keepdims=True)
        gt = g_ref[rows, :]
        o_ref[rows, :] = (o * lax.rsqrt(ms + NORM_EPS) * gain * (gt * _sigmoid(gt))).astype(o_ref.dtype)
    s_ref[...] = St

    @pl.when(t == pl.num_programs(2) - 1)
    def _():
        sout_ref[...] = s_ref[...].T


def hgrn2(qfig, lb, gain, s0, *, B, T, C, tb, t_valid=None):
    H = s0.shape[1]
    HD = H * HEAD_DIM
    x3 = qfig.reshape(B, T, 4 * HD)
    mats, masks = _hgrn_constants(C)
    n_levels = masks.shape[0] - 1
    kern = functools.partial(_hgrn_kernel, C=C, n_chunks=tb // C, n_levels=n_levels, t_valid=t_valid)

    def col(off):
        return pl.BlockSpec((None, tb, HEAD_DIM), lambda b, h, t: (b, t, h + off * H))

    head_vec = pl.BlockSpec((1, HEAD_DIM), lambda b, h, t: (0, h))
    o, s_out = pl.pallas_call(
        kern,
        grid=(B, H, T // tb),
        in_specs=[col(0), col(1), col(2), col(3), head_vec, head_vec,
                  pl.BlockSpec((None, None, HEAD_DIM, HEAD_DIM), lambda b, h, t: (b, h, 0, 0)),
                  pl.BlockSpec(mats.shape, lambda b, h, t: (0, 0)),
                  pl.BlockSpec(masks.shape, lambda b, h, t: (0, 0, 0))],
        out_specs=[pl.BlockSpec((None, tb, HEAD_DIM), lambda b, h, t: (b, t, h)),
                   pl.BlockSpec((None, None, HEAD_DIM, HEAD_DIM), lambda b, h, t: (b, h, 0, 0))],
        out_shape=[jax.ShapeDtypeStruct((B, T, HD), BF16),
                   jax.ShapeDtypeStruct(s0.shape, F32)],
        scratch_shapes=[pltpu.VMEM((HEAD_DIM, HEAD_DIM), F32)],
        compiler_params=_cparams(("arbitrary", "arbitrary", "arbitrary")),
        name="hgrn2",
    )(x3, x3, x3, x3, lb.reshape(1, HD), gain.reshape(1, HD), s0,
      jnp.asarray(mats, BF16), jnp.asarray(masks))
    return o.reshape(B * T, HD), s_out


def _log_sig_pair(z):
    lk = -jnp.maximum(z, 0.0) - jnp.log(1.0 + jnp.exp(-jnp.abs(z)))
    return z + lk, lk


def _suffix_sum(lk, u):
    hi = lk.astype(BF16)
    lo = (lk - hi.astype(F32)).astype(BF16)
    return _dot(hi, u) + _dot(lo, u)


def _sb_block(z, visible, u, rsum):
    ls, lk = _log_sig_pair(z)
    if visible is not None:
        lk = jnp.where(visible, lk, 0.0)
    la = _suffix_sum(lk, u) + rsum
    a = jnp.exp(ls + la)
    if visible is not None:
        a = jnp.where(visible, a, 0.0)
    return a, rsum + jnp.sum(lk, axis=-1, keepdims=True)


def _sb_prompt_kernel(bias_ref, q_ref, k_ref, v_ref, u_ref, o_ref, *, tq, tk, scale):
    h = pl.program_id(1)
    qi = pl.program_id(2)
    bias = bias_ref[h]
    q = q_ref[...].astype(BF16)
    u = u_ref[...]
    nd = tq // tk

    def block(kb, visible, rsum, acc):
        start = pl.multiple_of(kb * tk, tk)
        k = k_ref[pl.ds(start, tk), :].astype(BF16)
        v = v_ref[pl.ds(start, tk), :].astype(BF16)
        z = _dot_nt(q, k) * scale + bias
        a, rsum = _sb_block(z, visible, u, rsum)
        return rsum, acc + _dot(a.astype(BF16), v)

    row = lax.broadcasted_iota(jnp.int32, (tq, tk), 0)
    col = lax.broadcasted_iota(jnp.int32, (tq, tk), 1)
    carry = (jnp.zeros((tq, 1), F32), jnp.zeros((tq, HEAD_DIM), F32))
    for d in reversed(range(nd)):
        carry = block(qi * nd + d, col + d * tk < row, *carry)

    def body(it, carry):
        for d in range(nd):
            carry = block((qi - it) * nd - 1 - d, None, *carry)
        return carry

    rsum, acc = lax.fori_loop(0, qi, body, carry)
    o_ref[...] = acc.astype(o_ref.dtype)


def sb_attention_prompt(q, k, v, bias, *, B, T, tq=512, tk=256):
    N, HD = q.shape
    H = HD // HEAD_DIM
    tq = min(tq, T)
    tk = min(tk, tq)
    nq = T // tq
    u = jnp.asarray(np.triu(np.ones((tk, tk), np.float32), 0).T - np.eye(tk, dtype=np.float32), BF16)
    kern = functools.partial(_sb_prompt_kernel, tq=tq, tk=tk, scale=HEAD_DIM ** -0.5)
    return pl.pallas_call(
        kern,
        grid_spec=pltpu.PrefetchScalarGridSpec(
            num_scalar_prefetch=1,
            grid=(B, H, nq),
            in_specs=[pl.BlockSpec((tq, HEAD_DIM), lambda b, h, i, bias: (b * nq + i, h)),
                      pl.BlockSpec((T, HEAD_DIM), lambda b, h, i, bias: (b, h)),
                      pl.BlockSpec((T, HEAD_DIM), lambda b, h, i, bias: (b, h)),
                      pl.BlockSpec((tk, tk), lambda b, h, i, bias: (0, 0))],
            out_specs=pl.BlockSpec((tq, HEAD_DIM), lambda b, h, i, bias: (b * nq + i, h)),
        ),
        out_shape=jax.ShapeDtypeStruct((N, HD), BF16),
        compiler_params=_cparams(("arbitrary", "arbitrary", "arbitrary")),
        name="sb_attn_prompt",
    )(bias.astype(F32), q, k, v, u)


def _sb_sample_kernel(pt_ref, q_ref, bias_ref, u_ref, new_k, new_v, *rest, scale, G):
    pages = rest[:2 * G]
    o_ref, rsum_ref, acc_ref = rest[2 * G:]
    j = pl.program_id(1)
    H, tp, _ = q_ref.shape
    P = new_k.shape[0] // H
    R = H * tp

    @pl.when(j == 0)
    def _():
        rsum_ref[...] = jnp.zeros_like(rsum_ref)
        acc_ref[...] = jnp.zeros_like(acc_ref)

    def head_rows(ref, h):
        return ref[pl.ds(h, P, stride=H), :].astype(BF16)

    def block(k_ref, v_ref, visible, rsum, acc):
        z = jnp.concatenate([_dot_nt(q_ref[h], head_rows(k_ref, h)) for h in range(H)], axis=0)
        z = z * scale + bias_ref[...]
        a, rsum = _sb_block(z, visible, u_ref[...], rsum)
        acc = [acc[h] + _dot(a[h * tp:(h + 1) * tp, :].astype(BF16), head_rows(v_ref, h)) for h in range(H)]
        return rsum, acc

    def run(blocks):
        rsum = rsum_ref[...]
        acc = [acc_ref[h] for h in range(H)]
        for k_ref, v_ref, visible in blocks:
            rsum, acc = block(k_ref, v_ref, visible, rsum, acc)
        rsum_ref[...] = rsum
        for h in range(H):
            acc_ref[h] = acc[h]

    @pl.when(j == 0)
    def _():
        t_of_row = lax.broadcasted_iota(jnp.int32, (R, P), 0) % tp
        run([(new_k, new_v, lax.broadcasted_iota(jnp.int32, (R, P), 1) < t_of_row)])

    @pl.when(j > 0)
    def _():
        run([(pages[2 * g], pages[2 * g + 1], None) for g in range(G)])

    @pl.when(j == pl.num_programs(1) - 1)
    def _():
        o_ref[...] = acc_ref[...]


def sb_attention_sample(q, k_new, v_new, cache_k, cache_v, page_table, bias, *, B, T):
    HD = q.shape[1]
    H = HD // HEAD_DIM
    n_pages = page_table.shape[1]
    P = cache_k.shape[1]
    tp = SAMPLE_Q_ROWS
    R = H * tp
    qh = jnp.pad(q.reshape(B, T, H, HEAD_DIM).transpose(0, 2, 1, 3), ((0, 0), (0, 0), (0, tp - T), (0, 0)))
    bias_rows = jnp.repeat(bias.astype(F32), tp).reshape(R, 1)
    u = jnp.asarray(np.triu(np.ones((P, P), np.float32), 0).T - np.eye(P, dtype=np.float32), BF16)
    G = SAMPLE_PAGES_PER_STEP
    assert n_pages % G == 0
    page_view = lambda a: a.reshape(a.shape[0], P * H, HEAD_DIM)
    new_view = lambda a: page_view(jnp.pad(a.reshape(B, T, H, HEAD_DIM), ((0, 0), (0, P - T), (0, 0), (0, 0))))
    page = (None, P * H, HEAD_DIM)

    def page_spec(g):
        return pl.BlockSpec(page, lambda b, j, pt: (pt[b, n_pages - 1 - (jnp.maximum(j, 1) - 1) * G - g], 0, 0))

    in_specs = [pl.BlockSpec((None, H, tp, HEAD_DIM), lambda b, j, pt: (b, 0, 0, 0)),
                pl.BlockSpec((R, 1), lambda b, j, pt: (0, 0)),
                pl.BlockSpec((P, P), lambda b, j, pt: (0, 0)),
                pl.BlockSpec(page, lambda b, j, pt: (b, 0, 0)),
                pl.BlockSpec(page, lambda b, j, pt: (b, 0, 0))]
    args = [page_table, qh.astype(BF16), bias_rows, u, new_view(k_new), new_view(v_new)]
    ck, cv = page_view(cache_k), page_view(cache_v)
    for g in range(G):
        in_specs += [page_spec(g), page_spec(g)]
        args += [ck, cv]
    o = pl.pallas_call(
        functools.partial(_sb_sample_kernel, scale=HEAD_DIM ** -0.5, G=G),
        grid_spec=pltpu.PrefetchScalarGridSpec(
            num_scalar_prefetch=1,
            grid=(B, n_pages // G + 1),
            in_specs=in_specs,
            out_specs=pl.BlockSpec((None, H, tp, HEAD_DIM), lambda b, j, pt: (b, 0, 0, 0)),
            scratch_shapes=[pltpu.VMEM((R, 1), F32), pltpu.VMEM((H, tp, HEAD_DIM), F32)],
        ),
        out_shape=jax.ShapeDtypeStruct((B, H, tp, HEAD_DIM), F32),
        compiler_params=_cparams(("arbitrary", "arbitrary")),
        name="sb_attn_sample",
    )(*args)
    return o[:, :, :T].transpose(0, 2, 1, 3).reshape(B * T, HD).astype(BF16)


def _experts_kernel(be_ref, nb_ref, x_ref, wg_ref, wu_ref, wd_ref, o_ref, wgb_ref, wub_ref, wdb_ref):
    i = pl.program_id(0)
    prev = be_ref[jnp.maximum(i - 1, 0)]
    fresh = jnp.logical_or(i == 0, be_ref[i] != prev)

    @pl.when(fresh)
    def _():
        wgb_ref[...] = wg_ref[...].astype(BF16)
        wub_ref[...] = wu_ref[...].astype(BF16)
        wdb_ref[...] = wd_ref[...].astype(BF16)

    @pl.when(i < nb_ref[0])
    def _():
        x = x_ref[...]
        g = _dot(x, wgb_ref[...])
        up = _dot(x, wub_ref[...])
        a = (g * _sigmoid(g) * up).astype(BF16)
        o_ref[...] = _dot(a, wdb_ref[...]).astype(o_ref.dtype)

    @pl.when(i >= nb_ref[0])
    def _():
        o_ref[...] = jnp.zeros_like(o_ref)


def experts(x, block_e, n_used, wg, wu, wd, layer, tm, out_dtype, name):
    NR, D = x.shape
    Fd = wg.shape[-1]
    n_blocks = NR // tm
    return pl.pallas_call(
        _experts_kernel,
        grid_spec=pltpu.PrefetchScalarGridSpec(
            num_scalar_prefetch=2,
            grid=(n_blocks,),
            in_specs=[pl.BlockSpec((tm, D), lambda i, be, nb: (i, 0)),
                      pl.BlockSpec((None, None, D, Fd), lambda i, be, nb: (layer, be[i], 0, 0)),
                      pl.BlockSpec((None, None, D, Fd), lambda i, be, nb: (layer, be[i], 0, 0)),
                      pl.BlockSpec((None, None, Fd, D), lambda i, be, nb: (layer, be[i], 0, 0))],
            out_specs=pl.BlockSpec((tm, D), lambda i, be, nb: (i, 0)),
            scratch_shapes=[pltpu.VMEM((D, Fd), BF16), pltpu.VMEM((D, Fd), BF16),
                            pltpu.VMEM((Fd, D), BF16)],
        ),
        out_shape=jax.ShapeDtypeStruct((NR, D), out_dtype),
        compiler_params=_cparams(("arbitrary",)),
        name=name,
    )(block_e, n_used, x, wg, wu, wd)


def _combine_kernel(x_ref, gate_ref, sh_ref, wt_ref, g_ref, o_ref):
    wts = wt_ref[...]
    y = sh_ref[...]
    for k in range(TOP_K):
        y = y + wts[:, k:k + 1] * g_ref[k].astype(F32)
    o_ref[...] = x_ref[...] + gate_ref[...] * y


def combine(x, gate, shared, wts, gathered, rows_per_group, tm):
    N, D = x.shape
    tm = min(tm, N)
    if rows_per_group % tm == 0:
        bpg = rows_per_group // tm
        gate = gate.reshape(-1, 1, D)
        gate_spec = pl.BlockSpec((None, 1, D), lambda i: (i // bpg, 0, 0))
    else:
        gate = jnp.repeat(gate, rows_per_group, axis=0)
        gate_spec = pl.BlockSpec((tm, D), lambda i: (i, 0))
    row = pl.BlockSpec((tm, D), lambda i: (i, 0))
    return pl.pallas_call(
        _combine_kernel,
        grid=(N // tm,),
        in_specs=[row, gate_spec, row, pl.BlockSpec((tm, TOP_K), lambda i: (i, 0)),
                  pl.BlockSpec((TOP_K, tm, D), lambda i: (0, i, 0))],
        out_specs=row,
        out_shape=jax.ShapeDtypeStruct((N, D), F32),
        compiler_params=_cparams(("arbitrary",)),
        name="moe_combine",
    )(x, gate, shared, wts, gathered)


def moe_ffn(x, gate, rows_per_group, h, idx, wts, layer, exp_w_gate, exp_w_up, exp_w_down,
            sh_w_gate, sh_w_up, sh_w_down, tm_routed, tm_shared):
    N, D = h.shape

    tm = tm_routed
    NK = N * TOP_K
    n_blocks = -(-(NK + N_EXPERTS * (tm - 1)) // tm)
    member = jnp.sum(idx[:, :, None] == jnp.arange(N_EXPERTS)[None, None, :], axis=1).astype(jnp.int32)
    incl = jnp.cumsum(member, axis=0)
    counts = incl[-1]
    rank = jnp.take_along_axis(incl - member, idx, axis=1)
    padded = (counts + tm - 1) // tm * tm
    pend = jnp.cumsum(padded)
    pstart = pend - padded
    dest = pstart[idx] + rank
    tok = jnp.broadcast_to(jnp.arange(N, dtype=jnp.int32)[:, None], (N, TOP_K))
    rows = (jnp.arange(n_blocks * tm, dtype=jnp.int32) % N).at[dest.reshape(-1)].set(tok.reshape(-1))
    block_start = jnp.arange(n_blocks, dtype=jnp.int32) * tm
    block_e = jnp.minimum(jnp.sum(pend[None, :] <= block_start[:, None], axis=1), N_EXPERTS - 1).astype(jnp.int32)
    n_used = (pend[-1] // tm).astype(jnp.int32).reshape(1)
    xg = jnp.pad(h, ((0, N), (0, 0)))[rows]
    yb = experts(xg, block_e, n_used, exp_w_gate, exp_w_up, exp_w_down, layer, tm, BF16, "experts_routed")
    gathered = yb[dest.T.reshape(-1)].reshape(TOP_K, N, D)

    nsb = N // tm_shared
    shared = experts(h, jnp.zeros((nsb,), jnp.int32), jnp.full((1,), nsb, jnp.int32),
                     sh_w_gate[:, None], sh_w_up[:, None], sh_w_down[:, None], layer, tm_shared, F32,
                     "experts_shared")
    return combine(x, gate, shared, wts, gathered, rows_per_group, COMBINE_ROWS)


def trunk(x, mods, kv_mod, s0, past, p, cfg):
    B, T, D = x.shape
    N = B * T
    x = x.reshape(N, D)
    tm = cfg["tm"]
    HD = p["hg_w_out"].shape[1]
    lower = jnp.cumsum(jax.nn.softmax(p["hg_lb"].astype(F32), axis=0), axis=0)

    def split6(m):
        return [m[:, i * D:(i + 1) * D] for i in range(6)]

    sh_m, sc_m, g_m, sh_f, sc_f, g_f = split6(mods[0])
    h = norm_mod(x, p["norm_mix"][0], sh_m, sc_m, T, tm)
    mt = dict(tm=cfg["tm_mm"], tn=cfg["tn_mm"])
    qfig = mm(h, p["hg_w_in"], w_layer=0, name="hg_in", **mt)
    Tp = cfg["hg_t_pad"]
    if Tp != T:
        qfig = jnp.pad(qfig.reshape(B, T, -1), ((0, 0), (0, Tp - T), (0, 0))).reshape(B * Tp, -1)
    o, s_new = hgrn2(qfig, lower[0], p["hg_norm"][0], s0, B=B, T=Tp, C=cfg["hg_chunk"], tb=cfg["hg_tb"],
                     t_valid=None if Tp == T else T)
    if Tp != T:
        o = o.reshape(B, Tp, HD)[:, :T].reshape(N, HD)
    x = mm(o, p["hg_w_out"], w_layer=0, epilogue="resid", resid=x, gate=g_m, rows_per_group=T,
           name="hg_out", **mt)
    h, idx, wts = norm_mod(x, p["norm_ffn"][0], sh_f, sc_f, T, tm, router_w=p["router_w"],
                           router_bias=p["router_bias"], router_layer=0)
    x = moe_ffn(x, g_f, T, h, idx, wts, 0, p["exp_w_gate"], p["exp_w_up"], p["exp_w_down"],
                p["sh_w_gate"], p["sh_w_up"], p["sh_w_down"], cfg["tm_routed"], cfg["tm_shared"])

    sh_kv, sc_kv = kv_mod[:, :D], kv_mod[:, D:]
    hk = norm_mod(x, p["kv_norm"], sh_kv, sc_kv, T, tm)
    k_new = mm(hk, p["w_kv"], n_out=HD, epilogue="headnorm", head_gain=p["k_norm"], name="k_proj", **mt)
    v_new = mm(hk, p["w_kv"], w_col0=HD, n_out=HD, name="v_proj", **mt)

    sh_m, sc_m, g_m, sh_f, sc_f, g_f = split6(mods[1])
    h = norm_mod(x, p["norm_mix"][1], sh_m, sc_m, T, tm)
    q = mm(h, p["sb_w_q"], w_layer=0, epilogue="headnorm", head_gain=p["q_norm"][0], name="q_proj", **mt)
    bias = p["sb_logit_bias"][0]
    if past is None:
        o = sb_attention_prompt(q, k_new, v_new, bias, B=B, T=T)
    else:
        o = sb_attention_sample(q, k_new, v_new, past[0], past[1], past[2], bias, B=B, T=T)
    x = mm(o, p["sb_w_out"], w_layer=0, epilogue="resid", resid=x, gate=g_m, rows_per_group=T,
           name="sb_out", **mt)
    h, idx, wts = norm_mod(x, p["norm_ffn"][1], sh_f, sc_f, T, tm, router_w=p["router_w"],
                           router_bias=p["router_bias"], router_layer=1)
    x = moe_ffn(x, g_f, T, h, idx, wts, 1, p["exp_w_gate"], p["exp_w_up"], p["exp_w_down"],
                p["sh_w_gate"], p["sh_w_up"], p["sh_w_down"], cfg["tm_routed"], cfg["tm_shared"])
    H = HD // HEAD_DIM
    return (x.reshape(B, T, D), s_new[None], k_new.reshape(B, T, H, HEAD_DIM), v_new.reshape(B, T, H, HEAD_DIM))


PROMPT_CFG = dict(tm=512, tm_mm=1024, tn_mm=1024, hg_chunk=HG_CHUNK, hg_tb=512, hg_t_pad=None,
                  tm_routed=512, tm_shared=512)
SAMPLE_CFG = dict(tm=32, tm_mm=32, tn_mm=1024, hg_chunk=HG_CHUNK, hg_tb=HG_CHUNK, hg_t_pad=HG_CHUNK,
                  tm_routed=16, tm_shared=32)
MOD_ROWS = 16


def kernel(x_prompt, x_sample, c_prompt, c_sample, state_hgrn, cache_k, cache_v, page_table, ada_w, ada_b,
           norm_mix, norm_ffn, hg_w_in, hg_lb, hg_norm, hg_w_out, kv_ada_w, kv_ada_b, kv_norm, w_kv, k_norm,
           sb_w_q, q_norm, sb_logit_bias, sb_w_out, router_w, router_bias, exp_w_gate, exp_w_up, exp_w_down,
           sh_w_gate, sh_w_up, sh_w_down):
    p = dict(ada_w=ada_w, ada_b=ada_b, norm_mix=norm_mix, norm_ffn=norm_ffn, hg_w_in=hg_w_in, hg_lb=hg_lb,
             hg_norm=hg_norm, hg_w_out=hg_w_out, kv_ada_w=kv_ada_w, kv_ada_b=kv_ada_b, kv_norm=kv_norm,
             w_kv=w_kv, k_norm=k_norm, sb_w_q=sb_w_q, q_norm=q_norm, sb_logit_bias=sb_logit_bias,
             sb_w_out=sb_w_out, router_w=router_w, router_bias=router_bias, exp_w_gate=exp_w_gate,
             exp_w_up=exp_w_up, exp_w_down=exp_w_down, sh_w_gate=sh_w_gate, sh_w_up=sh_w_up,
             sh_w_down=sh_w_down)
    Bp, Tp, _ = x_prompt.shape
    Bs = x_sample.shape[0]
    H = hg_w_out.shape[1] // HEAD_DIM
    c_all = jnp.concatenate([c_prompt, c_sample], axis=0)
    c_all = jnp.pad(c_all, ((0, MOD_ROWS - c_all.shape[0]), (0, 0)))
    mods = [mm(c_all, ada_w, w_layer=l, bias=ada_b[l], silu_in=True, tn=1024, name="ada") for l in range(2)]
    kv_mod = mm(c_all, kv_ada_w, bias=kv_ada_b, silu_in=True, tn=1024, name="ada_kv")
    rows_p = slice(0, Bp)
    rows_s = slice(Bp, Bp + Bs)
    cfg_p = dict(PROMPT_CFG, hg_t_pad=Tp)
    s0_prompt = jnp.zeros((Bp, H, HEAD_DIM, HEAD_DIM), state_hgrn.dtype)
    y_p, s_p, k_p, v_p = trunk(x_prompt, [m[rows_p] for m in mods], kv_mod[rows_p], s0_prompt, None, p, cfg_p)
    y_s, s_s, k_s, v_s = trunk(x_sample, [m[rows_s] for m in mods], kv_mod[rows_s], state_hgrn[0],
                               (cache_k, cache_v, page_table), p, SAMPLE_CFG)
    return (y_p, y_s, s_p, s_s, k_p, v_p, k_s, v_s)
```

```python
import functools
import math

import numpy as np
import jax
import jax.numpy as jnp
from jax import lax
from jax.experimental import pallas as pl
from jax.experimental.pallas import tpu as pltpu

F32 = jnp.float32
BF16 = jnp.bfloat16

LANES = 128
SAMPLE_PAGES_PER_STEP = 4
VMEM_LIMIT_BYTES = 52 * 1024 * 1024
NORM_EPS = 1e-6
HEAD_DIM = 128
N_EXPERTS = 64
TOP_K = 8
ROUTED_SCALE = 2.5
HG_CHUNK = 64
COMBINE_ROWS = 256


def _cparams(sem):
    return pltpu.CompilerParams(dimension_semantics=sem, vmem_limit_bytes=VMEM_LIMIT_BYTES)


def _sigmoid(x):
    return 1.0 / (1.0 + jnp.exp(-x))


def _dot(a, b):
    return jnp.dot(a, b, preferred_element_type=F32)


def _dot_nt(a, b):
    return lax.dot_general(a, b, (((1,), (1,)), ((), ())), preferred_element_type=F32)


def _dot_tn(a, b):
    return lax.dot_general(a, b, (((0,), (0,)), ((), ())), preferred_element_type=F32)


def _norm_mod_kernel(x_ref, g_ref, sh_ref, sc_ref, *rest, with_router):
    x = x_ref[...]
    ms = jnp.mean(x * x, axis=-1, keepdims=True)
    y = x * lax.rsqrt(ms + NORM_EPS) * g_ref[...]
    h = y * (1.0 + sc_ref[...]) + sh_ref[...]
    if with_router:
        rw_ref, rb_ref, o_ref, idx_ref, wt_ref = rest
        logits = jnp.dot(h, rw_ref[...], preferred_element_type=F32, precision=lax.Precision.HIGHEST)
        scores = _sigmoid(logits)
        sel = scores + rb_ref[...]
        tm, E = sel.shape
        lane = lax.broadcasted_iota(jnp.int32, (tm, E), 1)
        slot = lax.broadcasted_iota(jnp.int32, (tm, TOP_K), 1)
        idx = jnp.zeros((tm, TOP_K), jnp.int32)
        wts = jnp.zeros((tm, TOP_K), F32)
        for k in range(TOP_K):
            best = jnp.max(sel, axis=-1, keepdims=True)
            pick = jnp.min(jnp.where(sel == best, lane, E), axis=-1, keepdims=True)
            hit = lane == pick
            w_k = jnp.sum(jnp.where(hit, scores, 0.0), axis=-1, keepdims=True)
            sel = jnp.where(hit, -jnp.inf, sel)
            idx = jnp.where(slot == k, pick, idx)
            wts = jnp.where(slot == k, w_k, wts)
        idx_ref[...] = idx
        wt_ref[...] = wts / jnp.sum(wts, axis=-1, keepdims=True) * ROUTED_SCALE
    else:
        o_ref, = rest
    o_ref[...] = h.astype(o_ref.dtype)


def norm_mod(x, gain, shift, scale, rows_per_group, tm, router_w=None, router_bias=None, router_layer=None):
    N, D = x.shape
    tm = min(tm, N)
    with_router = router_w is not None
    if rows_per_group % tm == 0:
        bpg = rows_per_group // tm
        sh = shift.reshape(-1, 1, D)
        sc = scale.reshape(-1, 1, D)
        mod_spec = pl.BlockSpec((None, 1, D), lambda i: (i // bpg, 0, 0))
    else:
        sh = jnp.repeat(shift, rows_per_group, axis=0)
        sc = jnp.repeat(scale, rows_per_group, axis=0)
        mod_spec = pl.BlockSpec((tm, D), lambda i: (i, 0))
    args = [x, gain.reshape(1, D), sh, sc]
    in_specs = [pl.BlockSpec((tm, D), lambda i: (i, 0)),
                pl.BlockSpec((1, D), lambda i: (0, 0)),
                mod_spec, mod_spec]
    out_specs = pl.BlockSpec((tm, D), lambda i: (i, 0))
    out_shape = jax.ShapeDtypeStruct((N, D), BF16)
    if with_router:
        E = router_w.shape[-1]
        args += [router_w, router_bias.astype(F32).reshape(-1, 1, E)]
        in_specs += [pl.BlockSpec((None, D, E), lambda i: (router_layer, 0, 0)),
                     pl.BlockSpec((None, 1, E), lambda i: (router_layer, 0, 0))]
        out_specs = [out_specs, pl.BlockSpec((tm, TOP_K), lambda i: (i, 0)),
                     pl.BlockSpec((tm, TOP_K), lambda i: (i, 0))]
        out_shape = [out_shape, jax.ShapeDtypeStruct((N, TOP_K), jnp.int32),
                     jax.ShapeDtypeStruct((N, TOP_K), F32)]
    return pl.pallas_call(
        functools.partial(_norm_mod_kernel, with_router=with_router),
        grid=(N // tm,),
        in_specs=in_specs,
        out_specs=out_specs,
        out_shape=out_shape,
        compiler_params=_cparams(("arbitrary",)),
        name="norm_router" if with_router else "norm_mod",
    )(*args)


def _mm_kernel(*refs, silu_in, has_bias, epilogue, tn):
    it = iter(refs)
    x_ref = next(it)
    w_ref = next(it)
    b_ref = next(it) if has_bias else None
    if epilogue == "headnorm":
        hg_ref = next(it)
    elif epilogue == "resid":
        r_ref = next(it)
        gate_ref = next(it)
    o_ref = next(it)
    wbf_ref = next(it)

    @pl.when(pl.program_id(1) == 0)
    def _():
        wbf_ref[...] = w_ref[...].astype(BF16)

    x = x_ref[...]
    if silu_in:
        x = x.astype(F32)
        x = x * _sigmoid(x)
    y = _dot(x.astype(BF16), wbf_ref[...])
    if has_bias:
        y = y + b_ref[...]
    if epilogue == "headnorm":
        gain = hg_ref[...]
        for c in range(tn // HEAD_DIM):
            sl = slice(c * HEAD_DIM, (c + 1) * HEAD_DIM)
            yc = y[:, sl]
            ms = jnp.mean(yc * yc, axis=-1, keepdims=True)
            o_ref[:, sl] = (yc * lax.rsqrt(ms + NORM_EPS) * gain).astype(o_ref.dtype)
    elif epilogue == "resid":
        o_ref[...] = (r_ref[...] + gate_ref[...] * y).astype(o_ref.dtype)
    else:
        o_ref[...] = y.astype(o_ref.dtype)


def mm(x, w, *, w_layer=None, w_col0=0, n_out=None, bias=None, silu_in=False, out_dtype=F32,
       tm=512, tn=512, epilogue=None, head_gain=None, resid=None, gate=None, rows_per_group=None,
       name="mm"):
    M, K = x.shape
    Ntot = w.shape[-1]
    N = Ntot - w_col0 if n_out is None else n_out
    tm = min(tm, M)
    tn = min(tn, N)
    assert M % tm == 0 and N % tn == 0 and w_col0 % tn == 0
    c0 = w_col0 // tn
    if w.ndim == 3:
        w_spec = pl.BlockSpec((None, K, tn), lambda j, i: (w_layer, 0, j + c0))
    else:
        w_spec = pl.BlockSpec((K, tn), lambda j, i: (0, j + c0))
    args = [x, w]
    in_specs = [pl.BlockSpec((tm, K), lambda j, i: (i, 0)), w_spec]
    if bias is not None:
        args.append(bias.reshape(1, -1))
        in_specs.append(pl.BlockSpec((1, tn), lambda j, i: (0, j + c0)))
    if epilogue == "headnorm":
        args.append(head_gain.reshape(1, HEAD_DIM))
        in_specs.append(pl.BlockSpec((1, HEAD_DIM), lambda j, i: (0, 0)))
    elif epilogue == "resid":
        args.append(resid)
        in_specs.append(pl.BlockSpec((tm, tn), lambda j, i: (i, j)))
        if rows_per_group % tm == 0:
            bpg = rows_per_group // tm
            args.append(gate.reshape(-1, 1, N))
            in_specs.append(pl.BlockSpec((None, 1, tn), lambda j, i: (i // bpg, 0, j)))
        else:
            args.append(jnp.repeat(gate, rows_per_group, axis=0))
            in_specs.append(pl.BlockSpec((tm, tn), lambda j, i: (i, j)))
    kern = functools.partial(_mm_kernel, silu_in=silu_in, has_bias=bias is not None,
                             epilogue=epilogue, tn=tn)
    return pl.pallas_call(
        kern,
        grid=(N // tn, M // tm),
        in_specs=in_specs,
        out_specs=pl.BlockSpec((tm, tn), lambda j, i: (i, j)),
        out_shape=jax.ShapeDtypeStruct((M, N), out_dtype),
        scratch_shapes=[pltpu.VMEM((K, tn), BF16)],
        compiler_params=_cparams(("arbitrary", "arbitrary")),
        name=name,
    )(*args)


def _hgrn_constants(C):
    r = np.arange(C)
    mats = [(r[None, :] <= r[:, None]).astype(np.float32)]
    masks = [np.eye(C, dtype=np.float32)]
    m = C // 2
    while m >= 1:
        base = (r // (2 * m)) * (2 * m)
        pivot = base + m - 1
        upper = r > pivot
        j = r[None, :]
        mat = np.where(upper[:, None], (j > pivot[:, None]) & (j <= r[:, None]),
                       (j > r[:, None]) & (j <= pivot[:, None]))
        mats.append(mat.astype(np.float32))
        same = base[:, None] == base[None, :]
        masks.append((same & upper[:, None] & (~upper)[None, :]).astype(np.float32))
        m //= 2
    return np.concatenate(mats, axis=0), np.stack(masks, axis=0)


def _hgrn_kernel(q_ref, fu_ref, i_ref, g_ref, lb_ref, gain_ref, s0_ref, mat_ref, mask_ref,
                 o_ref, sout_ref, s_ref, *, C, n_chunks, n_levels, t_valid):
    t = pl.program_id(2)

    @pl.when(t == 0)
    def _():
        s_ref[...] = s0_ref[...].T

    lb = lb_ref[...]
    gain = gain_ref[...]
    staged = []
    for c in range(n_chunks):
        rows = slice(c * C, (c + 1) * C)
        q = q_ref[rows, :]
        f = lb + (1.0 - lb) * _sigmoid(fu_ref[rows, :])
        logf = jnp.log(f)
        kk = 1.0 - f
        if t_valid is not None:
            valid = (lax.broadcasted_iota(jnp.int32, (C, HEAD_DIM), 0) + c * C) < t_valid
            logf = jnp.where(valid, logf, 0.0)
            kk = jnp.where(valid, kk, 0.0)
        hi = logf.astype(BF16)
        lo = (logf - hi.astype(F32)).astype(BF16)
        e = _dot(mat_ref[...], hi) + _dot(mat_ref[...], lo)
        b = e[0:C, :]
        scores = mask_ref[0] * _dot_nt(q.astype(BF16), kk.astype(BF16))
        for l in range(1, n_levels + 1):
            x = jnp.exp(e[l * C:(l + 1) * C, :])
            scores = scores + mask_ref[l] * _dot_nt((q * x).astype(BF16), (kk * x).astype(BF16))
        b_end = b[C - 1:C, :]
        staged.append(((q * jnp.exp(b)).astype(BF16), scores.astype(BF16), i_ref[rows, :].astype(BF16),
                       (kk * jnp.exp(b_end - b)).astype(BF16), jnp.exp(b_end)))

    St = s_ref[...]
    for c, (q_dec, scores, vb, khat, decay) in enumerate(staged):
        rows = slice(c * C, (c + 1) * C)
        o = _dot_nt(q_dec, St.astype(BF16)) + _dot(scores, vb)
        St = decay * St + _dot_tn(vb, khat)
        ms = jnp.mean(o * o, axis=-1, keepdims=True)
        gt = g_ref[rows, :]
        o_ref[rows, :] = (o * lax.rsqrt(ms + NORM_EPS) * gain * (gt * _sigmoid(gt))).astype(o_ref.dtype)
    s_ref[...] = St

    @pl.when(t == pl.num_programs(2) - 1)
    def _():
        sout_ref[...] = s_ref[...].T


def hgrn2(qfig, lb, gain, s0, *, B, T, C, tb, t_valid=None):
    H = s0.shape[1]
    HD = H * HEAD_DIM
    x3 = qfig.reshape(B, T, 4 * HD)
    mats, masks = _hgrn_constants(C)
    n_levels = masks.shape[0] - 1
    kern = functools.partial(_hgrn_kernel, C=C, n_chunks=tb // C, n_levels=n_levels, t_valid=t_valid)

    def col(off):
        return pl.BlockSpec((None, tb, HEAD_DIM), lambda b, h, t: (b, t, h + off * H))

    head_vec = pl.BlockSpec((1, HEAD_DIM), lambda b, h, t: (0, h))
    o, s_out = pl.pallas_call(
        kern,
        grid=(B, H, T // tb),
        in_specs=[col(0), col(1), col(2), col(3), head_vec, head_vec,
                  pl.BlockSpec((None, None, HEAD_DIM, HEAD_DIM), lambda b, h, t: (b, h, 0, 0)),
                  pl.BlockSpec(mats.shape, lambda b, h, t: (0, 0)),
                  pl.BlockSpec(masks.shape, lambda b, h, t: (0, 0, 0))],
        out_specs=[pl.BlockSpec((None, tb, HEAD_DIM), lambda b, h, t: (b, t, h)),
                   pl.BlockSpec((None, None, HEAD_DIM, HEAD_DIM), lambda b, h, t: (b, h, 0, 0))],
        out_shape=[jax.ShapeDtypeStruct((B, T, HD), BF16),
                   jax.ShapeDtypeStruct(s0.shape, F32)],
        scratch_shapes=[pltpu.VMEM((HEAD_DIM, HEAD_DIM), F32)],
        compiler_params=_cparams(("arbitrary", "arbitrary", "arbitrary")),
        name="hgrn2",
    )(x3, x3, x3, x3, lb.reshape(1, HD), gain.reshape(1, HD), s0,
      jnp.asarray(mats, BF16), jnp.asarray(masks))
    return o.reshape(B * T, HD), s_out


LOG2E = math.log2(math.e)
SIGN_BIT = 0x80000000


def _suffix_sum(x, u):
    hi = x.astype(BF16)
    lo = (x - hi.astype(F32)).astype(BF16)
    return _dot(hi, u) + _dot(lo, u)


def _sb_logs(w, visible):
    neg_abs = lax.bitcast_convert_type(lax.bitcast_convert_type(w, jnp.uint32) | jnp.uint32(SIGN_BIT), F32)
    drop = jnp.maximum(w, 0.0) + jnp.log(1.0 + jnp.exp2(neg_abs)) * LOG2E
    log_sig = w - drop
    if visible is not None:
        drop = jnp.where(visible, drop, 0.0)
    return log_sig, drop


def _sb_weights(log_sig, drop, visible, u, newer):
    a = jnp.exp2(log_sig - (_suffix_sum(drop, u) + newer))
    if visible is not None:
        a = jnp.where(visible, a, 0.0)
    return a


def _sb_block(w, visible, u, rsum):
    log_sig, drop = _sb_logs(w, visible)
    return _sb_weights(log_sig, drop, visible, u, rsum), rsum + jnp.sum(drop, axis=-1, keepdims=True)


def _sb_prompt_kernel(bias_ref, q_ref, k_ref, v_ref, u_ref, o_ref, *, tq, tk, scale):
    h = pl.program_id(1)
    qi = pl.program_id(2)
    bias2 = bias_ref[h] * LOG2E
    q = q_ref[...].astype(BF16)
    u = u_ref[...]
    nd = tq // tk

    def block(kb, visible, rsum, acc):
        start = pl.multiple_of(kb * tk, tk)
        k = k_ref[pl.ds(start, tk), :].astype(BF16)
        v = v_ref[pl.ds(start, tk), :].astype(BF16)
        w = _dot_nt(q, k) * (scale * LOG2E) + bias2
        a, rsum = _sb_block(w, visible, u, rsum)
        return rsum, acc + _dot(a.astype(BF16), v)

    row = lax.broadcasted_iota(jnp.int32, (tq, tk), 0)
    col = lax.broadcasted_iota(jnp.int32, (tq, tk), 1)
    carry = (jnp.zeros((tq, 1), F32), jnp.zeros((tq, HEAD_DIM), F32))
    for d in reversed(range(nd)):
        carry = block(qi * nd + d, col + d * tk < row, *carry)

    def body(it, carry):
        for d in range(nd):
            carry = block((qi - it) * nd - 1 - d, None, *carry)
        return carry

    rsum, acc = lax.fori_loop(0, qi, body, carry)
    o_ref[...] = acc.astype(o_ref.dtype)


def sb_attention_prompt(q, k, v, bias, *, B, T, tq=512, tk=256):
    N, HD = q.shape
    H = HD // HEAD_DIM
    tq = min(tq, T)
    tk = min(tk, tq)
    nq = T // tq
    u = jnp.asarray(np.triu(np.ones((tk, tk), np.float32), 0).T - np.eye(tk, dtype=np.float32), BF16)
    kern = functools.partial(_sb_prompt_kernel, tq=tq, tk=tk, scale=HEAD_DIM ** -0.5)
    return pl.pallas_call(
        kern,
        grid_spec=pltpu.PrefetchScalarGridSpec(
            num_scalar_prefetch=1,
            grid=(B, H, nq),
            in_specs=[pl.BlockSpec((tq, HEAD_DIM), lambda b, h, i, bias: (b * nq + i, h)),
                      pl.BlockSpec((T, HEAD_DIM), lambda b, h, i, bias: (b, h)),
                      pl.BlockSpec((T, HEAD_DIM), lambda b, h, i, bias: (b, h)),
                      pl.BlockSpec((tk, tk), lambda b, h, i, bias: (0, 0))],
            out_specs=pl.BlockSpec((tq, HEAD_DIM), lambda b, h, i, bias: (b * nq + i, h)),
        ),
        out_shape=jax.ShapeDtypeStruct((N, HD), BF16),
        compiler_params=_cparams(("arbitrary", "arbitrary", "arbitrary")),
        name="sb_attn_prompt",
    )(bias.astype(F32), q, k, v, u)


def _sb_sample_kernel(pt_ref, q_ref, bias_ref, u_ref, new_k, new_v, *rest, c1, G, tq):
    pages = rest[:2 * G]
    o_ref, rsum_ref, acc_ref = rest[2 * G:]
    j = pl.program_id(1)
    R = q_ref.shape[0]
    H = R // tq
    n_tiles = new_k.shape[0] // LANES
    keys_per_tile = LANES // H

    @pl.when(j == 0)
    def _():
        rsum_ref[...] = jnp.zeros_like(rsum_ref)
        acc_ref[...] = jnp.zeros_like(acc_ref)

    row = lax.broadcasted_iota(jnp.int32, (R, LANES), 0)
    lane = lax.broadcasted_iota(jnp.int32, (R, LANES), 1)
    own_head = (lane % H) == (row // tq)

    def stack(tiles):
        return jnp.concatenate(tiles, axis=0)

    def page(k_ref, v_ref, causal, rsum):
        w = _dot_nt(q_ref[...], k_ref[...].astype(BF16)) * c1 + bias_ref[...]
        ws = stack([w[:, t * LANES:(t + 1) * LANES] for t in range(n_tiles)])
        if causal:
            visible = stack([jnp.logical_and(own_head, t * keys_per_tile + lane // H < row % tq)
                             for t in range(n_tiles)])
        else:
            visible = stack([own_head] * n_tiles)
        log_sig, drop = _sb_logs(ws, visible)
        tile_sum = jnp.sum(drop, axis=-1, keepdims=True)
        newer = [None] * n_tiles
        for t in reversed(range(n_tiles)):
            newer[t] = rsum
            rsum = rsum + tile_sum[t * R:(t + 1) * R]
        a = _sb_weights(log_sig, drop, visible, u_ref[...], stack(newer)).astype(BF16)
        a = jnp.concatenate([a[t * R:(t + 1) * R] for t in range(n_tiles)], axis=1)
        return rsum, _dot(a, v_ref[...].astype(BF16))

    def run(blocks):
        rsum = rsum_ref[...]
        acc = acc_ref[...]
        for k_ref, v_ref, causal in blocks:
            rsum, o = page(k_ref, v_ref, causal, rsum)
            acc = acc + o
        rsum_ref[...] = rsum
        acc_ref[...] = acc

    @pl.when(j == 0)
    def _():
        run([(new_k, new_v, True)])

    @pl.when(j > 0)
    def _():
        run([(pages[2 * g], pages[2 * g + 1], False) for g in range(G)])

    @pl.when(j == pl.num_programs(1) - 1)
    def _():
        o_ref[...] = acc_ref[...]


def sb_attention_sample(q, k_new, v_new, cache_k, cache_v, page_table, bias, *, B, T):
    HD = q.shape[1]
    H = HD // HEAD_DIM
    n_pages = page_table.shape[1]
    P = cache_k.shape[1]
    R = H * T
    assert LANES % H == 0 and n_pages % SAMPLE_PAGES_PER_STEP == 0
    qh = q.reshape(B, T, H, HEAD_DIM).transpose(0, 2, 1, 3).reshape(B, R, HEAD_DIM)
    bias_rows = (jnp.repeat(bias.astype(F32), T) * LOG2E).reshape(R, 1)
    u = jnp.asarray(np.triu(np.ones((LANES, LANES), np.float32), 0).T - np.eye(LANES, dtype=np.float32), BF16)
    G = SAMPLE_PAGES_PER_STEP
    page_view = lambda a: a.reshape(a.shape[0], P * H, HEAD_DIM)
    new_view = lambda a: page_view(jnp.pad(a.reshape(B, T, H, HEAD_DIM), ((0, 0), (0, P - T), (0, 0), (0, 0))))
    page = (None, P * H, HEAD_DIM)

    def page_spec(g):
        return pl.BlockSpec(page, lambda b, j, pt: (pt[b, n_pages - 1 - (jnp.maximum(j, 1) - 1) * G - g], 0, 0))

    in_specs = [pl.BlockSpec((None, R, HEAD_DIM), lambda b, j, pt: (b, 0, 0)),
                pl.BlockSpec((R, 1), lambda b, j, pt: (0, 0)),
                pl.BlockSpec((LANES, LANES), lambda b, j, pt: (0, 0)),
                pl.BlockSpec(page, lambda b, j, pt: (b, 0, 0)),
                pl.BlockSpec(page, lambda b, j, pt: (b, 0, 0))]
    args = [page_table, qh.astype(BF16), bias_rows, u, new_view(k_new), new_view(v_new)]
    ck, cv = page_view(cache_k), page_view(cache_v)
    for g in range(G):
        in_specs += [page_spec(g), page_spec(g)]
        args += [ck, cv]
    o = pl.pallas_call(
        functools.partial(_sb_sample_kernel, c1=HEAD_DIM ** -0.5 * LOG2E, G=G, tq=T),
        grid_spec=pltpu.PrefetchScalarGridSpec(
            num_scalar_prefetch=1,
            grid=(B, n_pages // G + 1),
            in_specs=in_specs,
            out_specs=pl.BlockSpec((None, R, HEAD_DIM), lambda b, j, pt: (b, 0, 0)),
            scratch_shapes=[pltpu.VMEM((R, 1), F32), pltpu.VMEM((R, HEAD_DIM), F32)],
        ),
        out_shape=jax.ShapeDtypeStruct((B, R, HEAD_DIM), F32),
        compiler_params=_cparams(("arbitrary", "arbitrary")),
        name="sb_attn_sample",
    )(*args)
    return o.reshape(B, H, T, HEAD_DIM).transpose(0, 2, 1, 3).reshape(B * T, HD).astype(BF16)


def _experts_kernel(be_ref, nb_ref, first_ref, slot_ref, next_ref, x_ref, wg_hbm, wu_hbm, wd_hbm, o_ref,
                    wg_buf, wu_buf, wd_buf, wgb_ref, wub_ref, wdb_ref, sem, *, layer):
    i = pl.program_id(0)

    def weight_copies(e, s):
        return (pltpu.make_async_copy(wg_hbm.at[layer, e], wg_buf.at[s], sem.at[0, s]),
                pltpu.make_async_copy(wu_hbm.at[layer, e], wu_buf.at[s], sem.at[1, s]),
                pltpu.make_async_copy(wd_hbm.at[layer, e], wd_buf.at[s], sem.at[2, s]))

    @pl.when(i == 0)
    def _():
        for c in weight_copies(be_ref[0], 0):
            c.start()

    @pl.when(first_ref[i] == 1)
    def _():
        s = slot_ref[i]
        for c in weight_copies(be_ref[i], s):
            c.wait()
        wgb_ref[...] = wg_buf[s].astype(BF16)
        wub_ref[...] = wu_buf[s].astype(BF16)
        wdb_ref[...] = wd_buf[s].astype(BF16)

        @pl.when(next_ref[i] >= 0)
        def _():
            for c in weight_copies(next_ref[i], 1 - s):
                c.start()

    @pl.when(i < nb_ref[0])
    def _():
        x = x_ref[...]
        g = _dot(x, wgb_ref[...])
        up = _dot(x, wub_ref[...])
        a = (g * _sigmoid(g) * up).astype(BF16)
        o_ref[...] = _dot(a, wdb_ref[...]).astype(o_ref.dtype)

    @pl.when(i >= nb_ref[0])
    def _():
        o_ref[...] = jnp.zeros_like(o_ref)


def experts(x, block_e, n_used, wg, wu, wd, layer, tm, out_dtype, name):
    NR, D = x.shape
    Fd = wg.shape[-1]
    n_blocks = NR // tm
    pos = jnp.arange(n_blocks, dtype=jnp.int32)
    prev_e = jnp.concatenate([block_e[:1], block_e[:-1]])
    first = jnp.logical_and(pos < n_used[0], jnp.logical_or(pos == 0, block_e != prev_e))
    slot = (jnp.cumsum(first.astype(jnp.int32)) - 1) % 2
    first_pos = jnp.where(first, pos, n_blocks)
    next_first = lax.cummin(jnp.concatenate([first_pos[1:], jnp.full((1,), n_blocks, jnp.int32)]), reverse=True)
    next_e = jnp.where(next_first < n_blocks, block_e[jnp.minimum(next_first, n_blocks - 1)], -1)
    hbm = pl.BlockSpec(memory_space=pl.ANY)
    idx = lambda i, *_: (i, 0)
    return pl.pallas_call(
        functools.partial(_experts_kernel, layer=layer),
        grid_spec=pltpu.PrefetchScalarGridSpec(
            num_scalar_prefetch=5,
            grid=(n_blocks,),
            in_specs=[pl.BlockSpec((tm, D), idx), hbm, hbm, hbm],
            out_specs=pl.BlockSpec((tm, D), idx),
            scratch_shapes=[pltpu.VMEM((2, D, Fd), F32), pltpu.VMEM((2, D, Fd), F32), pltpu.VMEM((2, Fd, D), F32),
                            pltpu.VMEM((D, Fd), BF16), pltpu.VMEM((D, Fd), BF16), pltpu.VMEM((Fd, D), BF16),
                            pltpu.SemaphoreType.DMA((3, 2))],
        ),
        out_shape=jax.ShapeDtypeStruct((NR, D), out_dtype),
        compiler_params=_cparams(("arbitrary",)),
        name=name,
    )(block_e, n_used, first.astype(jnp.int32), slot.astype(jnp.int32), next_e.astype(jnp.int32), x, wg, wu, wd)


def _combine_kernel(x_ref, gate_ref, sh_ref, wt_ref, g_ref, o_ref):
    wts = wt_ref[...]
    y = sh_ref[...]
    for k in range(TOP_K):
        y = y + wts[:, k:k + 1] * g_ref[k].astype(F32)
    o_ref[...] = x_ref[...] + gate_ref[...] * y


def combine(x, gate, shared, wts, gathered, rows_per_group, tm):
    N, D = x.shape
    tm = min(tm, N)
    if rows_per_group % tm == 0:
        bpg = rows_per_group // tm
        gate = gate.reshape(-1, 1, D)
        gate_spec = pl.BlockSpec((None, 1, D), lambda i: (i // bpg, 0, 0))
    else:
        gate = jnp.repeat(gate, rows_per_group, axis=0)
        gate_spec = pl.BlockSpec((tm, D), lambda i: (i, 0))
    row = pl.BlockSpec((tm, D), lambda i: (i, 0))
    return pl.pallas_call(
        _combine_kernel,
        grid=(N // tm,),
        in_specs=[row, gate_spec, row, pl.BlockSpec((tm, TOP_K), lambda i: (i, 0)),
                  pl.BlockSpec((TOP_K, tm, D), lambda i: (0, i, 0))],
        out_specs=row,
        out_shape=jax.ShapeDtypeStruct((N, D), F32),
        compiler_params=_cparams(("arbitrary",)),
        name="moe_combine",
    )(x, gate, shared, wts, gathered)


def moe_ffn(x, gate, rows_per_group, h, idx, wts, layer, exp_w_gate, exp_w_up, exp_w_down,
            sh_w_gate, sh_w_up, sh_w_down, tm_routed, tm_shared):
    N, D = h.shape

    tm = tm_routed
    NK = N * TOP_K
    n_blocks = -(-(NK + N_EXPERTS * (tm - 1)) // tm)
    member = jnp.sum(idx[:, :, None] == jnp.arange(N_EXPERTS)[None, None, :], axis=1).astype(jnp.int32)
    incl = jnp.cumsum(member, axis=0)
    counts = incl[-1]
    rank = jnp.take_along_axis(incl - member, idx, axis=1)
    padded = (counts + tm - 1) // tm * tm
    pend = jnp.cumsum(padded)
    pstart = pend - padded
    dest = pstart[idx] + rank
    tok = jnp.broadcast_to(jnp.arange(N, dtype=jnp.int32)[:, None], (N, TOP_K))
    rows = (jnp.arange(n_blocks * tm, dtype=jnp.int32) % N).at[dest.reshape(-1)].set(tok.reshape(-1))
    block_start = jnp.arange(n_blocks, dtype=jnp.int32) * tm
    block_e = jnp.minimum(jnp.sum(pend[None, :] <= block_start[:, None], axis=1), N_EXPERTS - 1).astype(jnp.int32)
    n_used = (pend[-1] // tm).astype(jnp.int32).reshape(1)
    xg = jnp.pad(h, ((0, N), (0, 0)))[rows]
    yb = experts(xg, block_e, n_used, exp_w_gate, exp_w_up, exp_w_down, layer, tm, BF16, "experts_routed")
    gathered = yb[dest.T.reshape(-1)].reshape(TOP_K, N, D)

    nsb = N // tm_shared
    shared = experts(h, jnp.zeros((nsb,), jnp.int32), jnp.full((1,), nsb, jnp.int32),
                     sh_w_gate[:, None], sh_w_up[:, None], sh_w_down[:, None], layer, tm_shared, F32,
                     "experts_shared")
    return combine(x, gate, shared, wts, gathered, rows_per_group, COMBINE_ROWS)


def trunk(x, mods, kv_mod, s0, past, p, cfg):
    B, T, D = x.shape
    N = B * T
    x = x.reshape(N, D)
    tm = cfg["tm"]
    HD = p["hg_w_out"].shape[1]
    lower = jnp.cumsum(jax.nn.softmax(p["hg_lb"].astype(F32), axis=0), axis=0)

    def split6(m):
        return [m[:, i * D:(i + 1) * D] for i in range(6)]

    sh_m, sc_m, g_m, sh_f, sc_f, g_f = split6(mods[0])
    h = norm_mod(x, p["norm_mix"][0], sh_m, sc_m, T, tm)
    mt = dict(tm=cfg["tm_mm"], tn=cfg["tn_mm"])
    qfig = mm(h, p["hg_w_in"], w_layer=0, name="hg_in", **mt)
    Tp = cfg["hg_t_pad"]
    if Tp != T:
        qfig = jnp.pad(qfig.reshape(B, T, -1), ((0, 0), (0, Tp - T), (0, 0))).reshape(B * Tp, -1)
    o, s_new = hgrn2(qfig, lower[0], p["hg_norm"][0], s0, B=B, T=Tp, C=cfg["hg_chunk"], tb=cfg["hg_tb"],
                     t_valid=None if Tp == T else T)
    if Tp != T:
        o = o.reshape(B, Tp, HD)[:, :T].reshape(N, HD)
    x = mm(o, p["hg_w_out"], w_layer=0, epilogue="resid", resid=x, gate=g_m, rows_per_group=T,
           name="hg_out", **mt)
    h, idx, wts = norm_mod(x, p["norm_ffn"][0], sh_f, sc_f, T, tm, router_w=p["router_w"],
                           router_bias=p["router_bias"], router_layer=0)
    x = moe_ffn(x, g_f, T, h, idx, wts, 0, p["exp_w_gate"], p["exp_w_up"], p["exp_w_down"],
                p["sh_w_gate"], p["sh_w_up"], p["sh_w_down"], cfg["tm_routed"], cfg["tm_shared"])

    sh_kv, sc_kv = kv_mod[:, :D], kv_mod[:, D:]
    hk = norm_mod(x, p["kv_norm"], sh_kv, sc_kv, T, tm)
    k_new = mm(hk, p["w_kv"], n_out=HD, epilogue="headnorm", head_gain=p["k_norm"], name="k_proj", **mt)
    v_new = mm(hk, p["w_kv"], w_col0=HD, n_out=HD, name="v_proj", **mt)

    sh_m, sc_m, g_m, sh_f, sc_f, g_f = split6(mods[1])
    h = norm_mod(x, p["norm_mix"][1], sh_m, sc_m, T, tm)
    q = mm(h, p["sb_w_q"], w_layer=0, epilogue="headnorm", head_gain=p["q_norm"][0], name="q_proj", **mt)
    bias = p["sb_logit_bias"][0]
    if past is None:
        o = sb_attention_prompt(q, k_new, v_new, bias, B=B, T=T)
    else:
        o = sb_attention_sample(q, k_new, v_new, past[0], past[1], past[2], bias, B=B, T=T)
    x = mm(o, p["sb_w_out"], w_layer=0, epilogue="resid", resid=x, gate=g_m, rows_per_group=T,
           name="sb_out", **mt)
    h, idx, wts = norm_mod(x, p["norm_ffn"][1], sh_f, sc_f, T, tm, router_w=p["router_w"],
                           router_bias=p["router_bias"], router_layer=1)
    x = moe_ffn(x, g_f, T, h, idx, wts, 1, p["exp_w_gate"], p["exp_w_up"], p["exp_w_down"],
                p["sh_w_gate"], p["sh_w_up"], p["sh_w_down"], cfg["tm_routed"], cfg["tm_shared"])
    H = HD // HEAD_DIM
    return (x.reshape(B, T, D), s_new[None], k_new.reshape(B, T, H, HEAD_DIM), v_new.reshape(B, T, H, HEAD_DIM))


PROMPT_CFG = dict(tm=512, tm_mm=1024, tn_mm=1024, hg_chunk=HG_CHUNK, hg_tb=512, hg_t_pad=None,
                  tm_routed=512, tm_shared=512)
SAMPLE_CFG = dict(tm=32, tm_mm=32, tn_mm=1024, hg_chunk=HG_CHUNK, hg_tb=HG_CHUNK, hg_t_pad=HG_CHUNK,
                  tm_routed=16, tm_shared=32)
MOD_ROWS = 16


def kernel(x_prompt, x_sample, c_prompt, c_sample, state_hgrn, cache_k, cache_v, page_table, ada_w, ada_b,
           norm_mix, norm_ffn, hg_w_in, hg_lb, hg_norm, hg_w_out, kv_ada_w, kv_ada_b, kv_norm, w_kv, k_norm,
           sb_w_q, q_norm, sb_logit_bias, sb_w_out, router_w, router_bias, exp_w_gate, exp_w_up, exp_w_down,
           sh_w_gate, sh_w_up, sh_w_down):
    p = dict(ada_w=ada_w, ada_b=ada_b, norm_mix=norm_mix, norm_ffn=norm_ffn, hg_w_in=hg_w_in, hg_lb=hg_lb,
             hg_norm=hg_norm, hg_w_out=hg_w_out, kv_ada_w=kv_ada_w, kv_ada_b=kv_ada_b, kv_norm=kv_norm,
             w_kv=w_kv, k_norm=k_norm, sb_w_q=sb_w_q, q_norm=q_norm, sb_logit_bias=sb_logit_bias,
             sb_w_out=sb_w_out, router_w=router_w, router_bias=router_bias, exp_w_gate=exp_w_gate,
             exp_w_up=exp_w_up, exp_w_down=exp_w_down, sh_w_gate=sh_w_gate, sh_w_up=sh_w_up,
             sh_w_down=sh_w_down)
    Bp, Tp, _ = x_prompt.shape
    Bs = x_sample.shape[0]
    H = hg_w_out.shape[1] // HEAD_DIM
    c_all = jnp.concatenate([c_prompt, c_sample], axis=0)
    c_all = jnp.pad(c_all, ((0, MOD_ROWS - c_all.shape[0]), (0, 0)))
    mods = [mm(c_all, ada_w, w_layer=l, bias=ada_b[l], silu_in=True, tn=1024, name="ada") for l in range(2)]
    kv_mod = mm(c_all, kv_ada_w, bias=kv_ada_b, silu_in=True, tn=1024, name="ada_kv")
    rows_p = slice(0, Bp)
    rows_s = slice(Bp, Bp + Bs)
    cfg_p = dict(PROMPT_CFG, hg_t_pad=Tp)
    s0_prompt = jnp.zeros((Bp, H, HEAD_DIM, HEAD_DIM), state_hgrn.dtype)
    y_p, s_p, k_p, v_p = trunk(x_prompt, [m[rows_p] for m in mods], kv_mod[rows_p], s0_prompt, None, p, cfg_p)
    y_s, s_s, k_s, v_s = trunk(x_sample, [m[rows_s] for m in mods], kv_mod[rows_s], state_hgrn[0],
                               (cache_k, cache_v, page_table), p, SAMPLE_CFG)
    return (y_p, y_s, s_p, s_s, k_p, v_p, k_s, v_s)
```

```python
import functools
import math

import numpy as np
import jax
import jax.numpy as jnp
from jax import lax
from jax.experimental import pallas as pl
from jax.experimental.pallas import tpu as pltpu

F32 = jnp.float32
BF16 = jnp.bfloat16

LANES = 128
SAMPLE_PAGES_PER_STEP = 4
VMEM_LIMIT_BYTES = 52 * 1024 * 1024
NORM_EPS = 1e-6
HEAD_DIM = 128
N_EXPERTS = 64
TOP_K = 8
ROUTED_SCALE = 2.5
HG_CHUNK = 64
COMBINE_ROWS = 256


def _cparams(sem):
    return pltpu.CompilerParams(dimension_semantics=sem, vmem_limit_bytes=VMEM_LIMIT_BYTES)


def _sigmoid(x):
    return 1.0 / (1.0 + jnp.exp(-x))


def _dot(a, b):
    return jnp.dot(a, b, preferred_element_type=F32)


def _dot_nt(a, b):
    return lax.dot_general(a, b, (((1,), (1,)), ((), ())), preferred_element_type=F32)


def _dot_tn(a, b):
    return lax.dot_general(a, b, (((0,), (0,)), ((), ())), preferred_element_type=F32)


def _norm_mod_kernel(x_ref, g_ref, sh_ref, sc_ref, *rest, with_router):
    x = x_ref[...]
    ms = jnp.mean(x * x, axis=-1, keepdims=True)
    y = x * lax.rsqrt(ms + NORM_EPS) * g_ref[...]
    h = y * (1.0 + sc_ref[...]) + sh_ref[...]
    if with_router:
        rw_ref, rb_ref, o_ref, idx_ref, wt_ref = rest
        logits = jnp.dot(h, rw_ref[...], preferred_element_type=F32, precision=lax.Precision.HIGHEST)
        scores = _sigmoid(logits)
        sel = scores + rb_ref[...]
        tm, E = sel.shape
        lane = lax.broadcasted_iota(jnp.int32, (tm, E), 1)
        slot = lax.broadcasted_iota(jnp.int32, (tm, TOP_K), 1)
        idx = jnp.zeros((tm, TOP_K), jnp.int32)
        wts = jnp.zeros((tm, TOP_K), F32)
        for k in range(TOP_K):
            best = jnp.max(sel, axis=-1, keepdims=True)
            pick = jnp.min(jnp.where(sel == best, lane, E), axis=-1, keepdims=True)
            hit = lane == pick
            w_k = jnp.sum(jnp.where(hit, scores, 0.0), axis=-1, keepdims=True)
            sel = jnp.where(hit, -jnp.inf, sel)
            idx = jnp.where(slot == k, pick, idx)
            wts = jnp.where(slot == k, w_k, wts)
        idx_ref[...] = idx
        wt_ref[...] = wts / jnp.sum(wts, axis=-1, keepdims=True) * ROUTED_SCALE
    else:
        o_ref, = rest
    o_ref[...] = h.astype(o_ref.dtype)


def norm_mod(x, gain, shift, scale, rows_per_group, tm, router_w=None, router_bias=None, router_layer=None):
    N, D = x.shape
    tm = min(tm, N)
    with_router = router_w is not None
    if rows_per_group % tm == 0:
        bpg = rows_per_group // tm
        sh = shift.reshape(-1, 1, D)
        sc = scale.reshape(-1, 1, D)
        mod_spec = pl.BlockSpec((None, 1, D), lambda i: (i // bpg, 0, 0))
    else:
        sh = jnp.repeat(shift, rows_per_group, axis=0)
        sc = jnp.repeat(scale, rows_per_group, axis=0)
        mod_spec = pl.BlockSpec((tm, D), lambda i: (i, 0))
    args = [x, gain.reshape(1, D), sh, sc]
    in_specs = [pl.BlockSpec((tm, D), lambda i: (i, 0)),
                pl.BlockSpec((1, D), lambda i: (0, 0)),
                mod_spec, mod_spec]
    out_specs = pl.BlockSpec((tm, D), lambda i: (i, 0))
    out_shape = jax.ShapeDtypeStruct((N, D), BF16)
    if with_router:
        E = router_w.shape[-1]
        args += [router_w, router_bias.astype(F32).reshape(-1, 1, E)]
        in_specs += [pl.BlockSpec((None, D, E), lambda i: (router_layer, 0, 0)),
                     pl.BlockSpec((None, 1, E), lambda i: (router_layer, 0, 0))]
        out_specs = [out_specs, pl.BlockSpec((tm, TOP_K), lambda i: (i, 0)),
                     pl.BlockSpec((tm, TOP_K), lambda i: (i, 0))]
        out_shape = [out_shape, jax.ShapeDtypeStruct((N, TOP_K), jnp.int32),
                     jax.ShapeDtypeStruct((N, TOP_K), F32)]
    return pl.pallas_call(
        functools.partial(_norm_mod_kernel, with_router=with_router),
        grid=(N // tm,),
        in_specs=in_specs,
        out_specs=out_specs,
        out_shape=out_shape,
        compiler_params=_cparams(("arbitrary",)),
        name="norm_router" if with_router else "norm_mod",
    )(*args)


def _mm_kernel(*refs, silu_in, has_bias, epilogue, tn):
    it = iter(refs)
    x_ref = next(it)
    w_ref = next(it)
    b_ref = next(it) if has_bias else None
    if epilogue == "headnorm":
        hg_ref = next(it)
    elif epilogue == "resid":
        r_ref = next(it)
        gate_ref = next(it)
    o_ref = next(it)
    wbf_ref = next(it)

    @pl.when(pl.program_id(1) == 0)
    def _():
        wbf_ref[...] = w_ref[...].astype(BF16)

    x = x_ref[...]
    if silu_in:
        x = x.astype(F32)
        x = x * _sigmoid(x)
    y = _dot(x.astype(BF16), wbf_ref[...])
    if has_bias:
        y = y + b_ref[...]
    if epilogue == "headnorm":
        gain = hg_ref[...]
        for c in range(tn // HEAD_DIM):
            sl = slice(c * HEAD_DIM, (c + 1) * HEAD_DIM)
            yc = y[:, sl]
            ms = jnp.mean(yc * yc, axis=-1, keepdims=True)
            o_ref[:, sl] = (yc * lax.rsqrt(ms + NORM_EPS) * gain).astype(o_ref.dtype)
    elif epilogue == "resid":
        o_ref[...] = (r_ref[...] + gate_ref[...] * y).astype(o_ref.dtype)
    else:
        o_ref[...] = y.astype(o_ref.dtype)


def mm(x, w, *, w_layer=None, w_col0=0, n_out=None, bias=None, silu_in=False, out_dtype=F32,
       tm=512, tn=512, epilogue=None, head_gain=None, resid=None, gate=None, rows_per_group=None,
       name="mm"):
    M, K = x.shape
    Ntot = w.shape[-1]
    N = Ntot - w_col0 if n_out is None else n_out
    tm = min(tm, M)
    tn = min(tn, N)
    assert M % tm == 0 and N % tn == 0 and w_col0 % tn == 0
    c0 = w_col0 // tn
    if w.ndim == 3:
        w_spec = pl.BlockSpec((None, K, tn), lambda j, i: (w_layer, 0, j + c0))
    else:
        w_spec = pl.BlockSpec((K, tn), lambda j, i: (0, j + c0))
    args = [x, w]
    in_specs = [pl.BlockSpec((tm, K), lambda j, i: (i, 0)), w_spec]
    if bias is not None:
        args.append(bias.reshape(1, -1))
        in_specs.append(pl.BlockSpec((1, tn), lambda j, i: (0, j + c0)))
    if epilogue == "headnorm":
        args.append(head_gain.reshape(1, HEAD_DIM))
        in_specs.append(pl.BlockSpec((1, HEAD_DIM), lambda j, i: (0, 0)))
    elif epilogue == "resid":
        args.append(resid)
        in_specs.append(pl.BlockSpec((tm, tn), lambda j, i: (i, j)))
        if rows_per_group % tm == 0:
            bpg = rows_per_group // tm
            args.append(gate.reshape(-1, 1, N))
            in_specs.append(pl.BlockSpec((None, 1, tn), lambda j, i: (i // bpg, 0, j)))
        else:
            args.append(jnp.repeat(gate, rows_per_group, axis=0))
            in_specs.append(pl.BlockSpec((tm, tn), lambda j, i: (i, j)))
    kern = functools.partial(_mm_kernel, silu_in=silu_in, has_bias=bias is not None,
                             epilogue=epilogue, tn=tn)
    return pl.pallas_call(
        kern,
        grid=(N // tn, M // tm),
        in_specs=in_specs,
        out_specs=pl.BlockSpec((tm, tn), lambda j, i: (i, j)),
        out_shape=jax.ShapeDtypeStruct((M, N), out_dtype),
        scratch_shapes=[pltpu.VMEM((K, tn), BF16)],
        compiler_params=_cparams(("arbitrary", "arbitrary")),
        cost_estimate=pl.CostEstimate(flops=2 * M * K * N, transcendentals=0,
                                      bytes_accessed=(N // tn) * M * K * x.dtype.itemsize + 4 * K * N + 4 * M * N),
        name=name,
    )(*args)


def _hgrn_constants(C):
    r = np.arange(C)
    mats = [(r[None, :] <= r[:, None]).astype(np.float32)]
    masks = [np.eye(C, dtype=np.float32)]
    m = C // 2
    while m >= 1:
        base = (r // (2 * m)) * (2 * m)
        pivot = base + m - 1
        upper = r > pivot
        j = r[None, :]
        mat = np.where(upper[:, None], (j > pivot[:, None]) & (j <= r[:, None]),
                       (j > r[:, None]) & (j <= pivot[:, None]))
        mats.append(mat.astype(np.float32))
        same = base[:, None] == base[None, :]
        masks.append((same & upper[:, None] & (~upper)[None, :]).astype(np.float32))
        m //= 2
    return np.concatenate(mats, axis=0), np.stack(masks, axis=0)


def _hgrn_kernel(q_ref, fu_ref, i_ref, g_ref, lb_ref, gain_ref, s0_ref, mat_ref, mask_ref,
                 o_ref, sout_ref, s_ref, *, C, n_chunks, n_levels, t_valid):
    t = pl.program_id(2)

    @pl.when(t == 0)
    def _():
        s_ref[...] = s0_ref[...].T

    lb = lb_ref[...]
    gain = gain_ref[...]
    staged = []
    for c in range(n_chunks):
        rows = slice(c * C, (c + 1) * C)
        q = q_ref[rows, :]
        f = lb + (1.0 - lb) * _sigmoid(fu_ref[rows, :])
        logf = jnp.log(f)
        kk = 1.0 - f
        if t_valid is not None:
            valid = (lax.broadcasted_iota(jnp.int32, (C, HEAD_DIM), 0) + c * C) < t_valid
            logf = jnp.where(valid, logf, 0.0)
            kk = jnp.where(valid, kk, 0.0)
        hi = logf.astype(BF16)
        lo = (logf - hi.astype(F32)).astype(BF16)
        e = _dot(mat_ref[...], hi) + _dot(mat_ref[...], lo)
        b = e[0:C, :]
        scores = mask_ref[0] * _dot_nt(q.astype(BF16), kk.astype(BF16))
        for l in range(1, n_levels + 1):
            x = jnp.exp(e[l * C:(l + 1) * C, :])
            scores = scores + mask_ref[l] * _dot_nt((q * x).astype(BF16), (kk * x).astype(BF16))
        b_end = b[C - 1:C, :]
        staged.append(((q * jnp.exp(b)).astype(BF16), scores.astype(BF16), i_ref[rows, :].astype(BF16),
                       (kk * jnp.exp(b_end - b)).astype(BF16), jnp.exp(b_end)))

    St = s_ref[...]
    for c, (q_dec, scores, vb, khat, decay) in enumerate(staged):
        rows = slice(c * C, (c + 1) * C)
        o = _dot_nt(q_dec, St.astype(BF16)) + _dot(scores, vb)
        St = decay * St + _dot_tn(vb, khat)
        ms = jnp.mean(o * o, axis=-1, keepdims=True)
        gt = g_ref[rows, :]
        o_ref[rows, :] = (o * lax.rsqrt(ms + NORM_EPS) * gain * (gt * _sigmoid(gt))).astype(o_ref.dtype)
    s_ref[...] = St

    @pl.when(t == pl.num_programs(2) - 1)
    def _():
        sout_ref[...] = s_ref[...].T


def hgrn2(qfig, lb, gain, s0, *, B, T, C, tb, t_valid=None):
    H = s0.shape[1]
    HD = H * HEAD_DIM
    x3 = qfig.reshape(B, T, 4 * HD)
    mats, masks = _hgrn_constants(C)
    n_levels = masks.shape[0] - 1
    kern = functools.partial(_hgrn_kernel, C=C, n_chunks=tb // C, n_levels=n_levels, t_valid=t_valid)

    def col(off):
        return pl.BlockSpec((None, tb, HEAD_DIM), lambda b, h, t: (b, t, h + off * H))

    head_vec = pl.BlockSpec((1, HEAD_DIM), lambda b, h, t: (0, h))
    o, s_out = pl.pallas_call(
        kern,
        grid=(B, H, T // tb),
        in_specs=[col(0), col(1), col(2), col(3), head_vec, head_vec,
                  pl.BlockSpec((None, None, HEAD_DIM, HEAD_DIM), lambda b, h, t: (b, h, 0, 0)),
                  pl.BlockSpec(mats.shape, lambda b, h, t: (0, 0)),
                  pl.BlockSpec(masks.shape, lambda b, h, t: (0, 0, 0))],
        out_specs=[pl.BlockSpec((None, tb, HEAD_DIM), lambda b, h, t: (b, t, h)),
                   pl.BlockSpec((None, None, HEAD_DIM, HEAD_DIM), lambda b, h, t: (b, h, 0, 0))],
        out_shape=[jax.ShapeDtypeStruct((B, T, HD), BF16),
                   jax.ShapeDtypeStruct(s0.shape, F32)],
        scratch_shapes=[pltpu.VMEM((HEAD_DIM, HEAD_DIM), F32)],
        compiler_params=_cparams(("arbitrary", "arbitrary", "arbitrary")),
        cost_estimate=pl.CostEstimate(flops=2 * B * T * H * HEAD_DIM * (3 * C * (n_levels + 1) + 4 * HEAD_DIM),
                                      transcendentals=B * T * HD * (n_levels + 8),
                                      bytes_accessed=B * T * HD * 18),
        name="hgrn2",
    )(x3, x3, x3, x3, lb.reshape(1, HD), gain.reshape(1, HD), s0,
      jnp.asarray(mats, BF16), jnp.asarray(masks))
    return o.reshape(B * T, HD), s_out


LOG2E = math.log2(math.e)
SIGN_BIT = 0x80000000


def _suffix_sum(x, u):
    hi = x.astype(BF16)
    lo = (x - hi.astype(F32)).astype(BF16)
    return _dot(hi, u) + _dot(lo, u)


def _sb_logs(w, visible):
    neg_abs = lax.bitcast_convert_type(lax.bitcast_convert_type(w, jnp.uint32) | jnp.uint32(SIGN_BIT), F32)
    drop = jnp.maximum(w, 0.0) + jnp.log(1.0 + jnp.exp2(neg_abs)) * LOG2E
    log_sig = w - drop
    if visible is not None:
        drop = jnp.where(visible, drop, 0.0)
    return log_sig, drop


def _sb_weights(log_sig, drop, visible, u, newer):
    a = jnp.exp2(log_sig - (_suffix_sum(drop, u) + newer))
    if visible is not None:
        a = jnp.where(visible, a, 0.0)
    return a


def _sb_block(w, visible, u, rsum):
    log_sig, drop = _sb_logs(w, visible)
    return _sb_weights(log_sig, drop, visible, u, rsum), rsum + jnp.sum(drop, axis=-1, keepdims=True)


def _sb_prompt_kernel(bias_ref, q_ref, k_ref, v_ref, u_ref, o_ref, *, tq, tk, scale):
    h = pl.program_id(1)
    qi = pl.program_id(2)
    bias2 = bias_ref[h] * LOG2E
    q = q_ref[...].astype(BF16)
    u = u_ref[...]
    nd = tq // tk

    def block(kb, visible, rsum, acc):
        start = pl.multiple_of(kb * tk, tk)
        k = k_ref[pl.ds(start, tk), :].astype(BF16)
        v = v_ref[pl.ds(start, tk), :].astype(BF16)
        w = _dot_nt(q, k) * (scale * LOG2E) + bias2
        a, rsum = _sb_block(w, visible, u, rsum)
        return rsum, acc + _dot(a.astype(BF16), v)

    row = lax.broadcasted_iota(jnp.int32, (tq, tk), 0)
    col = lax.broadcasted_iota(jnp.int32, (tq, tk), 1)
    carry = (jnp.zeros((tq, 1), F32), jnp.zeros((tq, HEAD_DIM), F32))
    for d in reversed(range(nd)):
        carry = block(qi * nd + d, col + d * tk < row, *carry)

    def body(it, carry):
        for d in range(nd):
            carry = block((qi - it) * nd - 1 - d, None, *carry)
        return carry

    rsum, acc = lax.fori_loop(0, qi, body, carry)
    o_ref[...] = acc.astype(o_ref.dtype)


def sb_attention_prompt(q, k, v, bias, *, B, T, tq=512, tk=256):
    N, HD = q.shape
    H = HD // HEAD_DIM
    tq = min(tq, T)
    tk = min(tk, tq)
    nq = T // tq
    u = jnp.asarray(np.triu(np.ones((tk, tk), np.float32), 0).T - np.eye(tk, dtype=np.float32), BF16)
    kern = functools.partial(_sb_prompt_kernel, tq=tq, tk=tk, scale=HEAD_DIM ** -0.5)
    return pl.pallas_call(
        kern,
        grid_spec=pltpu.PrefetchScalarGridSpec(
            num_scalar_prefetch=1,
            grid=(B, H, nq),
            in_specs=[pl.BlockSpec((tq, HEAD_DIM), lambda b, h, i, bias: (b * nq + i, h)),
                      pl.BlockSpec((T, HEAD_DIM), lambda b, h, i, bias: (b, h)),
                      pl.BlockSpec((T, HEAD_DIM), lambda b, h, i, bias: (b, h)),
                      pl.BlockSpec((tk, tk), lambda b, h, i, bias: (0, 0))],
            out_specs=pl.BlockSpec((tq, HEAD_DIM), lambda b, h, i, bias: (b * nq + i, h)),
        ),
        out_shape=jax.ShapeDtypeStruct((N, HD), BF16),
        compiler_params=_cparams(("arbitrary", "arbitrary", "arbitrary")),
        cost_estimate=pl.CostEstimate(flops=B * H * T * T * (2 * HEAD_DIM + 2 * tk), transcendentals=3 * B * H * T * T // 2,
                                      bytes_accessed=N * HD * (2 * 4 + 4 + 2)),
        name="sb_attn_prompt",
    )(bias.astype(F32), q, k, v, u)


def _sb_sample_kernel(pt_ref, q_ref, bias_ref, u_ref, new_k, new_v, *rest, c1, G, tq):
    pages = rest[:2 * G]
    o_ref, rsum_ref, acc_ref = rest[2 * G:]
    j = pl.program_id(1)
    R = q_ref.shape[0]
    H = R // tq
    n_tiles = new_k.shape[0] // LANES
    keys_per_tile = LANES // H

    @pl.when(j == 0)
    def _():
        rsum_ref[...] = jnp.zeros_like(rsum_ref)
        acc_ref[...] = jnp.zeros_like(acc_ref)

    row = lax.broadcasted_iota(jnp.int32, (R, LANES), 0)
    lane = lax.broadcasted_iota(jnp.int32, (R, LANES), 1)
    own_head = (lane % H) == (row // tq)

    def stack(tiles):
        return jnp.concatenate(tiles, axis=0)

    def page(k_ref, v_ref, causal, rsum):
        w = _dot_nt(q_ref[...], k_ref[...].astype(BF16)) * c1 + bias_ref[...]
        ws = stack([w[:, t * LANES:(t + 1) * LANES] for t in range(n_tiles)])
        if causal:
            visible = stack([jnp.logical_and(own_head, t * keys_per_tile + lane // H < row % tq)
                             for t in range(n_tiles)])
        else:
            visible = stack([own_head] * n_tiles)
        log_sig, drop = _sb_logs(ws, visible)
        tile_sum = jnp.sum(drop, axis=-1, keepdims=True)
        newer = [None] * n_tiles
        for t in reversed(range(n_tiles)):
            newer[t] = rsum
            rsum = rsum + tile_sum[t * R:(t + 1) * R]
        a = _sb_weights(log_sig, drop, visible, u_ref[...], stack(newer)).astype(BF16)
        a = jnp.concatenate([a[t * R:(t + 1) * R] for t in range(n_tiles)], axis=1)
        return rsum, _dot(a, v_ref[...].astype(BF16))

    def run(blocks):
        rsum = rsum_ref[...]
        acc = acc_ref[...]
        for k_ref, v_ref, causal in blocks:
            rsum, o = page(k_ref, v_ref, causal, rsum)
            acc = acc + o
        rsum_ref[...] = rsum
        acc_ref[...] = acc

    @pl.when(j == 0)
    def _():
        run([(new_k, new_v, True)])

    @pl.when(j > 0)
    def _():
        run([(pages[2 * g], pages[2 * g + 1], False) for g in range(G)])

    @pl.when(j == pl.num_programs(1) - 1)
    def _():
        o_ref[...] = acc_ref[...]


def sb_attention_sample(q, k_new, v_new, cache_k, cache_v, page_table, bias, *, B, T):
    HD = q.shape[1]
    H = HD // HEAD_DIM
    n_pages = page_table.shape[1]
    P = cache_k.shape[1]
    R = H * T
    assert LANES % H == 0 and n_pages % SAMPLE_PAGES_PER_STEP == 0
    qh = q.reshape(B, T, H, HEAD_DIM).transpose(0, 2, 1, 3).reshape(B, R, HEAD_DIM)
    bias_rows = (jnp.repeat(bias.astype(F32), T) * LOG2E).reshape(R, 1)
    u = jnp.asarray(np.triu(np.ones((LANES, LANES), np.float32), 0).T - np.eye(LANES, dtype=np.float32), BF16)
    G = SAMPLE_PAGES_PER_STEP
    page_view = lambda a: a.reshape(a.shape[0], P * H, HEAD_DIM)
    new_view = lambda a: page_view(jnp.pad(a.reshape(B, T, H, HEAD_DIM), ((0, 0), (0, P - T), (0, 0), (0, 0))))
    page = (None, P * H, HEAD_DIM)

    def page_spec(g):
        return pl.BlockSpec(page, lambda b, j, pt: (pt[b, n_pages - 1 - (jnp.maximum(j, 1) - 1) * G - g], 0, 0))

    in_specs = [pl.BlockSpec((None, R, HEAD_DIM), lambda b, j, pt: (b, 0, 0)),
                pl.BlockSpec((R, 1), lambda b, j, pt: (0, 0)),
                pl.BlockSpec((LANES, LANES), lambda b, j, pt: (0, 0)),
                pl.BlockSpec(page, lambda b, j, pt: (b, 0, 0)),
                pl.BlockSpec(page, lambda b, j, pt: (b, 0, 0))]
    args = [page_table, qh.astype(BF16), bias_rows, u, new_view(k_new), new_view(v_new)]
    ck, cv = page_view(cache_k), page_view(cache_v)
    for g in range(G):
        in_specs += [page_spec(g), page_spec(g)]
        args += [ck, cv]
    o = pl.pallas_call(
        functools.partial(_sb_sample_kernel, c1=HEAD_DIM ** -0.5 * LOG2E, G=G, tq=T),
        grid_spec=pltpu.PrefetchScalarGridSpec(
            num_scalar_prefetch=1,
            grid=(B, n_pages // G + 1),
            in_specs=in_specs,
            out_specs=pl.BlockSpec((None, R, HEAD_DIM), lambda b, j, pt: (b, 0, 0)),
            scratch_shapes=[pltpu.VMEM((R, 1), F32), pltpu.VMEM((R, HEAD_DIM), F32)],
        ),
        out_shape=jax.ShapeDtypeStruct((B, R, HEAD_DIM), F32),
        compiler_params=_cparams(("arbitrary", "arbitrary")),
        cost_estimate=pl.CostEstimate(flops=B * (n_pages + 1) * P * H * R * (4 * HEAD_DIM + 4 * LANES),
                                      transcendentals=3 * B * (n_pages + 1) * P * H * R,
                                      bytes_accessed=2 * B * (n_pages + 1) * P * HD * 4),
        name="sb_attn_sample",
    )(*args)
    return o.reshape(B, H, T, HEAD_DIM).transpose(0, 2, 1, 3).reshape(B * T, HD).astype(BF16)


def _experts_kernel(be_ref, nb_ref, first_ref, slot_ref, next_ref, x_ref, wg_hbm, wu_hbm, wd_hbm, o_ref,
                    wg_buf, wu_buf, wd_buf, wgb_ref, wub_ref, wdb_ref, sem, *, layer):
    i = pl.program_id(0)

    def weight_copies(e, s):
        return (pltpu.make_async_copy(wg_hbm.at[layer, e], wg_buf.at[s], sem.at[0, s]),
                pltpu.make_async_copy(wu_hbm.at[layer, e], wu_buf.at[s], sem.at[1, s]),
                pltpu.make_async_copy(wd_hbm.at[layer, e], wd_buf.at[s], sem.at[2, s]))

    @pl.when(i == 0)
    def _():
        for c in weight_copies(be_ref[0], 0):
            c.start()

    @pl.when(first_ref[i] == 1)
    def _():
        s = slot_ref[i]
        for c in weight_copies(be_ref[i], s):
            c.wait()

        @pl.when(next_ref[i] >= 0)
        def _():
            for c in weight_copies(next_ref[i], 1 - s):
                c.start()

        wgb_ref[...] = wg_buf[s].astype(BF16)
        wub_ref[...] = wu_buf[s].astype(BF16)
        wdb_ref[...] = wd_buf[s].astype(BF16)

    @pl.when(i < nb_ref[0])
    def _():
        x = x_ref[...]
        g = _dot(x, wgb_ref[...])
        up = _dot(x, wub_ref[...])
        a = (g * _sigmoid(g) * up).astype(BF16)
        o_ref[...] = _dot(a, wdb_ref[...]).astype(o_ref.dtype)

    @pl.when(i >= nb_ref[0])
    def _():
        o_ref[...] = jnp.zeros_like(o_ref)


def experts(x, block_e, n_used, wg, wu, wd, layer, tm, out_dtype, name):
    NR, D = x.shape
    Fd = wg.shape[-1]
    n_blocks = NR // tm
    pos = jnp.arange(n_blocks, dtype=jnp.int32)
    prev_e = jnp.concatenate([block_e[:1], block_e[:-1]])
    first = jnp.logical_and(pos < n_used[0], jnp.logical_or(pos == 0, block_e != prev_e))
    slot = (jnp.cumsum(first.astype(jnp.int32)) - 1) % 2
    first_pos = jnp.where(first, pos, n_blocks)
    next_first = lax.cummin(jnp.concatenate([first_pos[1:], jnp.full((1,), n_blocks, jnp.int32)]), reverse=True)
    next_e = jnp.where(next_first < n_blocks, block_e[jnp.minimum(next_first, n_blocks - 1)], -1)
    hbm = pl.BlockSpec(memory_space=pl.ANY)
    idx = lambda i, *_: (i, 0)
    return pl.pallas_call(
        functools.partial(_experts_kernel, layer=layer),
        grid_spec=pltpu.PrefetchScalarGridSpec(
            num_scalar_prefetch=5,
            grid=(n_blocks,),
            in_specs=[pl.BlockSpec((tm, D), idx), hbm, hbm, hbm],
            out_specs=pl.BlockSpec((tm, D), idx),
            scratch_shapes=[pltpu.VMEM((2, D, Fd), F32), pltpu.VMEM((2, D, Fd), F32), pltpu.VMEM((2, Fd, D), F32),
                            pltpu.VMEM((D, Fd), BF16), pltpu.VMEM((D, Fd), BF16), pltpu.VMEM((Fd, D), BF16),
                            pltpu.SemaphoreType.DMA((3, 2))],
        ),
        out_shape=jax.ShapeDtypeStruct((NR, D), out_dtype),
        compiler_params=_cparams(("arbitrary",)),
        cost_estimate=pl.CostEstimate(flops=6 * NR * D * Fd, transcendentals=NR * Fd,
                                      bytes_accessed=min(n_blocks, wg.shape[1]) * 12 * D * Fd + 4 * NR * D),
        name=name,
    )(block_e, n_used, first.astype(jnp.int32), slot.astype(jnp.int32), next_e.astype(jnp.int32), x, wg, wu, wd)


def _combine_kernel(x_ref, gate_ref, sh_ref, wt_ref, g_ref, o_ref):
    wts = wt_ref[...]
    y = sh_ref[...]
    for k in range(TOP_K):
        y = y + wts[:, k:k + 1] * g_ref[k].astype(F32)
    o_ref[...] = x_ref[...] + gate_ref[...] * y


def combine(x, gate, shared, wts, gathered, rows_per_group, tm):
    N, D = x.shape
    tm = min(tm, N)
    if rows_per_group % tm == 0:
        bpg = rows_per_group // tm
        gate = gate.reshape(-1, 1, D)
        gate_spec = pl.BlockSpec((None, 1, D), lambda i: (i // bpg, 0, 0))
    else:
        gate = jnp.repeat(gate, rows_per_group, axis=0)
        gate_spec = pl.BlockSpec((tm, D), lambda i: (i, 0))
    row = pl.BlockSpec((tm, D), lambda i: (i, 0))
    return pl.pallas_call(
        _combine_kernel,
        grid=(N // tm,),
        in_specs=[row, gate_spec, row, pl.BlockSpec((tm, TOP_K), lambda i: (i, 0)),
                  pl.BlockSpec((TOP_K, tm, D), lambda i: (0, i, 0))],
        out_specs=row,
        out_shape=jax.ShapeDtypeStruct((N, D), F32),
        compiler_params=_cparams(("arbitrary",)),
        name="moe_combine",
    )(x, gate, shared, wts, gathered)


def moe_dispatch(h, idx, tm):
    N, D = h.shape
    NK = N * TOP_K
    n_blocks = -(-(NK + N_EXPERTS * (tm - 1)) // tm)
    member = jnp.sum(idx[:, :, None] == jnp.arange(N_EXPERTS)[None, None, :], axis=1).astype(jnp.int32)
    incl = jnp.cumsum(member, axis=0)
    counts = incl[-1]
    rank = jnp.take_along_axis(incl - member, idx, axis=1)
    padded = (counts + tm - 1) // tm * tm
    pend = jnp.cumsum(padded)
    pstart = pend - padded
    dest = pstart[idx] + rank
    tok = jnp.broadcast_to(jnp.arange(N, dtype=jnp.int32)[:, None], (N, TOP_K))
    rows = (jnp.arange(n_blocks * tm, dtype=jnp.int32) % N).at[dest.reshape(-1)].set(tok.reshape(-1))
    block_start = jnp.arange(n_blocks, dtype=jnp.int32) * tm
    block_e = jnp.minimum(jnp.sum(pend[None, :] <= block_start[:, None], axis=1), N_EXPERTS - 1).astype(jnp.int32)
    n_used = (pend[-1] // tm).astype(jnp.int32).reshape(1)
    xg = jnp.pad(h, ((0, N), (0, 0)))[rows]
    return xg, block_e, n_used, dest


def moe_experts(h, dispatched, layer, p, tm_routed, tm_shared):
    N, D = h.shape
    xg, block_e, n_used, dest = dispatched
    yb = experts(xg, block_e, n_used, p["exp_w_gate"], p["exp_w_up"], p["exp_w_down"], layer, tm_routed, BF16,
                 "experts_routed")
    gathered = yb[dest.T.reshape(-1)].reshape(TOP_K, N, D)
    nsb = N // tm_shared
    shared = experts(h, jnp.zeros((nsb,), jnp.int32), jnp.full((1,), nsb, jnp.int32),
                     p["sh_w_gate"][:, None], p["sh_w_up"][:, None], p["sh_w_down"][:, None], layer, tm_shared,
                     F32, "experts_shared")
    return shared, gathered


def _zero_after(token):
    return lax.shift_right_arithmetic(token, jnp.int32(31))


def moe_stages(x, gate, rows_per_group, h, idx, wts, layer, p, cfg):
    dispatched = moe_dispatch(h, idx, cfg["tm_routed"])
    xg, block_e, n_used, dest = dispatched
    other = yield block_e[0]
    if cfg["anchor"]:
        dispatched = (xg, block_e, n_used + _zero_after(other), dest)
    shared, gathered = moe_experts(h, dispatched, layer, p, cfg["tm_routed"], cfg["tm_shared"])
    probe = shared[0, 0] + gathered[0, 0, 0].astype(F32)
    other = yield (probe != probe).astype(jnp.int32)
    if cfg["anchor"]:
        wts = wts + _zero_after(other).astype(F32)
    return combine(x, gate, shared, wts, gathered, rows_per_group, COMBINE_ROWS)


def trunk(x, mods, kv_mod, s0, past, p, cfg):
    B, T, D = x.shape
    N = B * T
    x = x.reshape(N, D)
    tm = cfg["tm"]
    HD = p["hg_w_out"].shape[1]
    lower = jnp.cumsum(jax.nn.softmax(p["hg_lb"].astype(F32), axis=0), axis=0)

    def split6(m):
        return [m[:, i * D:(i + 1) * D] for i in range(6)]

    sh_m, sc_m, g_m, sh_f, sc_f, g_f = split6(mods[0])
    h = norm_mod(x, p["norm_mix"][0], sh_m, sc_m, T, tm)
    mt = dict(tm=cfg["tm_mm"], tn=cfg["tn_mm"])
    qfig = mm(h, p["hg_w_in"], w_layer=0, name="hg_in", **mt)
    Tp = cfg["hg_t_pad"]
    if Tp != T:
        qfig = jnp.pad(qfig.reshape(B, T, -1), ((0, 0), (0, Tp - T), (0, 0))).reshape(B * Tp, -1)
    o, s_new = hgrn2(qfig, lower[0], p["hg_norm"][0], s0, B=B, T=Tp, C=cfg["hg_chunk"], tb=cfg["hg_tb"],
                     t_valid=None if Tp == T else T)
    if Tp != T:
        o = o.reshape(B, Tp, HD)[:, :T].reshape(N, HD)
    x = mm(o, p["hg_w_out"], w_layer=0, epilogue="resid", resid=x, gate=g_m, rows_per_group=T,
           name="hg_out", **mt)
    h, idx, wts = norm_mod(x, p["norm_ffn"][0], sh_f, sc_f, T, tm, router_w=p["router_w"],
                           router_bias=p["router_bias"], router_layer=0)
    x = yield from moe_stages(x, g_f, T, h, idx, wts, 0, p, cfg)

    sh_kv, sc_kv = kv_mod[:, :D], kv_mod[:, D:]
    hk = norm_mod(x, p["kv_norm"], sh_kv, sc_kv, T, tm)
    k_new = mm(hk, p["w_kv"], n_out=HD, epilogue="headnorm", head_gain=p["k_norm"], name="k_proj", **mt)
    v_new = mm(hk, p["w_kv"], w_col0=HD, n_out=HD, name="v_proj", **mt)

    sh_m, sc_m, g_m, sh_f, sc_f, g_f = split6(mods[1])
    h = norm_mod(x, p["norm_mix"][1], sh_m, sc_m, T, tm)
    q = mm(h, p["sb_w_q"], w_layer=0, epilogue="headnorm", head_gain=p["q_norm"][0], name="q_proj", **mt)
    bias = p["sb_logit_bias"][0]
    if past is None:
        o = sb_attention_prompt(q, k_new, v_new, bias, B=B, T=T)
    else:
        o = sb_attention_sample(q, k_new, v_new, past[0], past[1], past[2], bias, B=B, T=T)
    x = mm(o, p["sb_w_out"], w_layer=0, epilogue="resid", resid=x, gate=g_m, rows_per_group=T,
           name="sb_out", **mt)
    h, idx, wts = norm_mod(x, p["norm_ffn"][1], sh_f, sc_f, T, tm, router_w=p["router_w"],
                           router_bias=p["router_bias"], router_layer=1)
    x = yield from moe_stages(x, g_f, T, h, idx, wts, 1, p, cfg)
    H = HD // HEAD_DIM
    return (x.reshape(B, T, D), s_new[None], k_new.reshape(B, T, H, HEAD_DIM), v_new.reshape(B, T, H, HEAD_DIM))


PROMPT_CFG = dict(tm=512, tm_mm=1024, tn_mm=1024, hg_chunk=HG_CHUNK, hg_tb=512, hg_t_pad=None,
                  tm_routed=512, tm_shared=512, anchor=True)
SAMPLE_CFG = dict(tm=32, tm_mm=32, tn_mm=1024, hg_chunk=HG_CHUNK, hg_tb=HG_CHUNK, hg_t_pad=HG_CHUNK,
                  tm_routed=16, tm_shared=32, anchor=False)
MOD_ROWS = 16


def kernel(x_prompt, x_sample, c_prompt, c_sample, state_hgrn, cache_k, cache_v, page_table, ada_w, ada_b,
           norm_mix, norm_ffn, hg_w_in, hg_lb, hg_norm, hg_w_out, kv_ada_w, kv_ada_b, kv_norm, w_kv, k_norm,
           sb_w_q, q_norm, sb_logit_bias, sb_w_out, router_w, router_bias, exp_w_gate, exp_w_up, exp_w_down,
           sh_w_gate, sh_w_up, sh_w_down):
    p = dict(ada_w=ada_w, ada_b=ada_b, norm_mix=norm_mix, norm_ffn=norm_ffn, hg_w_in=hg_w_in, hg_lb=hg_lb,
             hg_norm=hg_norm, hg_w_out=hg_w_out, kv_ada_w=kv_ada_w, kv_ada_b=kv_ada_b, kv_norm=kv_norm,
             w_kv=w_kv, k_norm=k_norm, sb_w_q=sb_w_q, q_norm=q_norm, sb_logit_bias=sb_logit_bias,
             sb_w_out=sb_w_out, router_w=router_w, router_bias=router_bias, exp_w_gate=exp_w_gate,
             exp_w_up=exp_w_up, exp_w_down=exp_w_down, sh_w_gate=sh_w_gate, sh_w_up=sh_w_up,
             sh_w_down=sh_w_down)
    Bp, Tp, _ = x_prompt.shape
    Bs = x_sample.shape[0]
    H = hg_w_out.shape[1] // HEAD_DIM
    c_all = jnp.concatenate([c_prompt, c_sample], axis=0)
    c_all = jnp.pad(c_all, ((0, MOD_ROWS - c_all.shape[0]), (0, 0)))
    mods = [mm(c_all, ada_w, w_layer=l, bias=ada_b[l], silu_in=True, tn=1024, name="ada") for l in range(2)]
    kv_mod = mm(c_all, kv_ada_w, bias=kv_ada_b, silu_in=True, tn=1024, name="ada_kv")
    rows_p = slice(0, Bp)
    rows_s = slice(Bp, Bp + Bs)
    cfg_p = dict(PROMPT_CFG, hg_t_pad=Tp)
    s0_prompt = jnp.zeros((Bp, H, HEAD_DIM, HEAD_DIM), state_hgrn.dtype)
    groups = [trunk(x_prompt, [m[rows_p] for m in mods], kv_mod[rows_p], s0_prompt, None, p, cfg_p),
              trunk(x_sample, [m[rows_s] for m in mods], kv_mod[rows_s], state_hgrn[0],
                    (cache_k, cache_v, page_table), p, SAMPLE_CFG)]
    results = [None, None]
    tokens = [None, None]
    while any(r is None for r in results):
        for g, gen in enumerate(groups):
            if results[g] is None:
                try:
                    tokens[g] = gen.send(tokens[1 - g]) if tokens[g] is not None else next(gen)
                except StopIteration as done:
                    results[g] = done.value
    (y_p, s_p, k_p, v_p), (y_s, s_s, k_s, v_s) = results
    return (y_p, y_s, s_p, s_s, k_p, v_p, k_s, v_s)
```

```python
import functools
import math

import numpy as np
import jax
import jax.numpy as jnp
from jax import lax
from jax.experimental import pallas as pl
from jax.experimental.pallas import tpu as pltpu

F32 = jnp.float32
BF16 = jnp.bfloat16

LANES = 128
SAMPLE_PAGES_PER_STEP = 4
VMEM_LIMIT_BYTES = 52 * 1024 * 1024
NORM_EPS = 1e-6
HEAD_DIM = 128
N_EXPERTS = 64
TOP_K = 8
ROUTED_SCALE = 2.5
HG_CHUNK = 64
COMBINE_ROWS = 256
ROUTED_ROWS = 512


def _cparams(sem):
    return pltpu.CompilerParams(dimension_semantics=sem, vmem_limit_bytes=VMEM_LIMIT_BYTES)


def _sigmoid(x):
    return 1.0 / (1.0 + jnp.exp(-x))


def _dot(a, b):
    return jnp.dot(a, b, preferred_element_type=F32)


def _dot_nt(a, b):
    return lax.dot_general(a, b, (((1,), (1,)), ((), ())), preferred_element_type=F32)


def _dot_tn(a, b):
    return lax.dot_general(a, b, (((0,), (0,)), ((), ())), preferred_element_type=F32)


def _norm_mod_kernel(x_ref, g_ref, sh_ref, sc_ref, *rest, with_router):
    x = x_ref[...]
    ms = jnp.mean(x * x, axis=-1, keepdims=True)
    y = x * lax.rsqrt(ms + NORM_EPS) * g_ref[...]
    h = y * (1.0 + sc_ref[...]) + sh_ref[...]
    if with_router:
        rw_ref, rb_ref, o_ref, idx_ref, wt_ref = rest
        logits = jnp.dot(h, rw_ref[...], preferred_element_type=F32, precision=lax.Precision.HIGHEST)
        scores = _sigmoid(logits)
        sel = scores + rb_ref[...]
        tm, E = sel.shape
        lane = lax.broadcasted_iota(jnp.int32, (tm, E), 1)
        slot = lax.broadcasted_iota(jnp.int32, (tm, TOP_K), 1)
        idx = jnp.zeros((tm, TOP_K), jnp.int32)
        wts = jnp.zeros((tm, TOP_K), F32)
        for k in range(TOP_K):
            best = jnp.max(sel, axis=-1, keepdims=True)
            pick = jnp.min(jnp.where(sel == best, lane, E), axis=-1, keepdims=True)
            hit = lane == pick
            w_k = jnp.sum(jnp.where(hit, scores, 0.0), axis=-1, keepdims=True)
            sel = jnp.where(hit, -jnp.inf, sel)
            idx = jnp.where(slot == k, pick, idx)
            wts = jnp.where(slot == k, w_k, wts)
        idx_ref[...] = idx
        wt_ref[...] = wts / jnp.sum(wts, axis=-1, keepdims=True) * ROUTED_SCALE
    else:
        o_ref, = rest
    o_ref[...] = h.astype(o_ref.dtype)


def norm_mod(x, gain, shift, scale, rows_per_group, tm, router_w=None, router_bias=None, router_layer=None):
    N, D = x.shape
    tm = min(tm, N)
    with_router = router_w is not None
    if rows_per_group % tm == 0:
        bpg = rows_per_group // tm
        sh = shift.reshape(-1, 1, D)
        sc = scale.reshape(-1, 1, D)
        mod_spec = pl.BlockSpec((None, 1, D), lambda i: (i // bpg, 0, 0))
    else:
        sh = jnp.repeat(shift, rows_per_group, axis=0)
        sc = jnp.repeat(scale, rows_per_group, axis=0)
        mod_spec = pl.BlockSpec((tm, D), lambda i: (i, 0))
    args = [x, gain.reshape(1, D), sh, sc]
    in_specs = [pl.BlockSpec((tm, D), lambda i: (i, 0)),
                pl.BlockSpec((1, D), lambda i: (0, 0)),
                mod_spec, mod_spec]
    out_specs = pl.BlockSpec((tm, D), lambda i: (i, 0))
    out_shape = jax.ShapeDtypeStruct((N, D), BF16)
    if with_router:
        E = router_w.shape[-1]
        args += [router_w, router_bias.astype(F32).reshape(-1, 1, E)]
        in_specs += [pl.BlockSpec((None, D, E), lambda i: (router_layer, 0, 0)),
                     pl.BlockSpec((None, 1, E), lambda i: (router_layer, 0, 0))]
        out_specs = [out_specs, pl.BlockSpec((tm, TOP_K), lambda i: (i, 0)),
                     pl.BlockSpec((tm, TOP_K), lambda i: (i, 0))]
        out_shape = [out_shape, jax.ShapeDtypeStruct((N, TOP_K), jnp.int32),
                     jax.ShapeDtypeStruct((N, TOP_K), F32)]
    return pl.pallas_call(
        functools.partial(_norm_mod_kernel, with_router=with_router),
        grid=(N // tm,),
        in_specs=in_specs,
        out_specs=out_specs,
        out_shape=out_shape,
        compiler_params=_cparams(("arbitrary",)),
        name="norm_router" if with_router else "norm_mod",
    )(*args)


def _mm_kernel(*refs, silu_in, has_bias, epilogue, tn):
    it = iter(refs)
    x_ref = next(it)
    w_ref = next(it)
    b_ref = next(it) if has_bias else None
    if epilogue == "headnorm":
        hg_ref = next(it)
    elif epilogue == "resid":
        r_ref = next(it)
        gate_ref = next(it)
    o_ref = next(it)
    wbf_ref = next(it)

    @pl.when(pl.program_id(1) == 0)
    def _():
        wbf_ref[...] = w_ref[...].astype(BF16)

    x = x_ref[...]
    if silu_in:
        x = x.astype(F32)
        x = x * _sigmoid(x)
    y = _dot(x.astype(BF16), wbf_ref[...])
    if has_bias:
        y = y + b_ref[...]
    if epilogue == "headnorm":
        gain = hg_ref[...]
        for c in range(tn // HEAD_DIM):
            sl = slice(c * HEAD_DIM, (c + 1) * HEAD_DIM)
            yc = y[:, sl]
            ms = jnp.mean(yc * yc, axis=-1, keepdims=True)
            o_ref[:, sl] = (yc * lax.rsqrt(ms + NORM_EPS) * gain).astype(o_ref.dtype)
    elif epilogue == "resid":
        o_ref[...] = (r_ref[...] + gate_ref[...] * y).astype(o_ref.dtype)
    else:
        o_ref[...] = y.astype(o_ref.dtype)


def mm(x, w, *, w_layer=None, w_col0=0, n_out=None, bias=None, silu_in=False, out_dtype=F32,
       tm=512, tn=512, epilogue=None, head_gain=None, resid=None, gate=None, rows_per_group=None,
       name="mm"):
    M, K = x.shape
    Ntot = w.shape[-1]
    N = Ntot - w_col0 if n_out is None else n_out
    tm = min(tm, M)
    tn = min(tn, N)
    assert M % tm == 0 and N % tn == 0 and w_col0 % tn == 0
    c0 = w_col0 // tn
    if w.ndim == 3:
        w_spec = pl.BlockSpec((None, K, tn), lambda j, i: (w_layer, 0, j + c0))
    else:
        w_spec = pl.BlockSpec((K, tn), lambda j, i: (0, j + c0))
    args = [x, w]
    in_specs = [pl.BlockSpec((tm, K), lambda j, i: (i, 0)), w_spec]
    if bias is not None:
        args.append(bias.reshape(1, -1))
        in_specs.append(pl.BlockSpec((1, tn), lambda j, i: (0, j + c0)))
    if epilogue == "headnorm":
        args.append(head_gain.reshape(1, HEAD_DIM))
        in_specs.append(pl.BlockSpec((1, HEAD_DIM), lambda j, i: (0, 0)))
    elif epilogue == "resid":
        args.append(resid)
        in_specs.append(pl.BlockSpec((tm, tn), lambda j, i: (i, j)))
        if rows_per_group % tm == 0:
            bpg = rows_per_group // tm
            args.append(gate.reshape(-1, 1, N))
            in_specs.append(pl.BlockSpec((None, 1, tn), lambda j, i: (i // bpg, 0, j)))
        else:
            args.append(jnp.repeat(gate, rows_per_group, axis=0))
            in_specs.append(pl.BlockSpec((tm, tn), lambda j, i: (i, j)))
    kern = functools.partial(_mm_kernel, silu_in=silu_in, has_bias=bias is not None,
                             epilogue=epilogue, tn=tn)
    return pl.pallas_call(
        kern,
        grid=(N // tn, M // tm),
        in_specs=in_specs,
        out_specs=pl.BlockSpec((tm, tn), lambda j, i: (i, j)),
        out_shape=jax.ShapeDtypeStruct((M, N), out_dtype),
        scratch_shapes=[pltpu.VMEM((K, tn), BF16)],
        compiler_params=_cparams(("arbitrary", "arbitrary")),
        cost_estimate=pl.CostEstimate(flops=2 * M * K * N, transcendentals=0,
                                      bytes_accessed=(N // tn) * M * K * x.dtype.itemsize + 4 * K * N + 4 * M * N),
        name=name,
    )(*args)


def _hgrn_constants(C):
    r = np.arange(C)
    mats = [(r[None, :] <= r[:, None]).astype(np.float32)]
    masks = [np.eye(C, dtype=np.float32)]
    m = C // 2
    while m >= 1:
        base = (r // (2 * m)) * (2 * m)
        pivot = base + m - 1
        upper = r > pivot
        j = r[None, :]
        mat = np.where(upper[:, None], (j > pivot[:, None]) & (j <= r[:, None]),
                       (j > r[:, None]) & (j <= pivot[:, None]))
        mats.append(mat.astype(np.float32))
        same = base[:, None] == base[None, :]
        masks.append((same & upper[:, None] & (~upper)[None, :]).astype(np.float32))
        m //= 2
    return np.concatenate(mats, axis=0), np.stack(masks, axis=0)


def _hgrn_kernel(q_ref, fu_ref, i_ref, g_ref, lb_ref, gain_ref, s0_ref, mat_ref, mask_ref,
                 o_ref, sout_ref, s_ref, *, C, n_chunks, n_levels, t_valid):
    t = pl.program_id(2)

    @pl.when(t == 0)
    def _():
        s_ref[...] = s0_ref[...].T

    lb = lb_ref[...]
    gain = gain_ref[...]
    staged = []
    for c in range(n_chunks):
        rows = slice(c * C, (c + 1) * C)
        q = q_ref[rows, :]
        f = lb + (1.0 - lb) * _sigmoid(fu_ref[rows, :])
        logf = jnp.log(f)
        kk = 1.0 - f
        if t_valid is not None:
            valid = (lax.broadcasted_iota(jnp.int32, (C, HEAD_DIM), 0) + c * C) < t_valid
            logf = jnp.where(valid, logf, 0.0)
            kk = jnp.where(valid, kk, 0.0)
        hi = logf.astype(BF16)
        lo = (logf - hi.astype(F32)).astype(BF16)
        e = _dot(mat_ref[...], hi) + _dot(mat_ref[...], lo)
        b = e[0:C, :]
        scores = mask_ref[0] * _dot_nt(q.astype(BF16), kk.astype(BF16))
        for l in range(1, n_levels + 1):
            x = jnp.exp(e[l * C:(l + 1) * C, :])
            scores = scores + mask_ref[l] * _dot_nt((q * x).astype(BF16), (kk * x).astype(BF16))
        b_end = b[C - 1:C, :]
        staged.append(((q * jnp.exp(b)).astype(BF16), scores.astype(BF16), i_ref[rows, :].astype(BF16),
                       (kk * jnp.exp(b_end - b)).astype(BF16), jnp.exp(b_end)))

    St = s_ref[...]
    for c, (q_dec, scores, vb, khat, decay) in enumerate(staged):
        rows = slice(c * C, (c + 1) * C)
        o = _dot_nt(q_dec, St.astype(BF16)) + _dot(scores, vb)
        St = decay * St + _dot_tn(vb, khat)
        ms = jnp.mean(o * o, axis=-1, keepdims=True)
        gt = g_ref[rows, :]
        o_ref[rows, :] = (o * lax.rsqrt(ms + NORM_EPS) * gain * (gt * _sigmoid(gt))).astype(o_ref.dtype)
    s_ref[...] = St

    @pl.when(t == pl.num_programs(2) - 1)
    def _():
        sout_ref[...] = s_ref[...].T


def hgrn2(qfig, lb, gain, s0, *, B, T, C, tb, t_valid=None):
    H = s0.shape[1]
    HD = H * HEAD_DIM
    x3 = qfig.reshape(B, T, 4 * HD)
    mats, masks = _hgrn_constants(C)
    n_levels = masks.shape[0] - 1
    kern = functools.partial(_hgrn_kernel, C=C, n_chunks=tb // C, n_levels=n_levels, t_valid=t_valid)

    def col(off):
        return pl.BlockSpec((None, tb, HEAD_DIM), lambda b, h, t: (b, t, h + off * H))

    head_vec = pl.BlockSpec((1, HEAD_DIM), lambda b, h, t: (0, h))
    o, s_out = pl.pallas_call(
        kern,
        grid=(B, H, T // tb),
        in_specs=[col(0), col(1), col(2), col(3), head_vec, head_vec,
                  pl.BlockSpec((None, None, HEAD_DIM, HEAD_DIM), lambda b, h, t: (b, h, 0, 0)),
                  pl.BlockSpec(mats.shape, lambda b, h, t: (0, 0)),
                  pl.BlockSpec(masks.shape, lambda b, h, t: (0, 0, 0))],
        out_specs=[pl.BlockSpec((None, tb, HEAD_DIM), lambda b, h, t: (b, t, h)),
                   pl.BlockSpec((None, None, HEAD_DIM, HEAD_DIM), lambda b, h, t: (b, h, 0, 0))],
        out_shape=[jax.ShapeDtypeStruct((B, T, HD), BF16),
                   jax.ShapeDtypeStruct(s0.shape, F32)],
        scratch_shapes=[pltpu.VMEM((HEAD_DIM, HEAD_DIM), F32)],
        compiler_params=_cparams(("arbitrary", "arbitrary", "arbitrary")),
        cost_estimate=pl.CostEstimate(flops=2 * B * T * H * HEAD_DIM * (3 * C * (n_levels + 1) + 4 * HEAD_DIM),
                                      transcendentals=B * T * HD * (n_levels + 8),
                                      bytes_accessed=B * T * HD * 18),
        name="hgrn2",
    )(x3, x3, x3, x3, lb.reshape(1, HD), gain.reshape(1, HD), s0,
      jnp.asarray(mats, BF16), jnp.asarray(masks))
    return o.reshape(B * T, HD), s_out


LOG2E = math.log2(math.e)
SIGN_BIT = 0x80000000


def _suffix_sum(x, u):
    hi = x.astype(BF16)
    lo = (x - hi.astype(F32)).astype(BF16)
    return _dot(hi, u) + _dot(lo, u)


def _sb_logs(w, visible):
    neg_abs = lax.bitcast_convert_type(lax.bitcast_convert_type(w, jnp.uint32) | jnp.uint32(SIGN_BIT), F32)
    drop = jnp.maximum(w, 0.0) + jnp.log(1.0 + jnp.exp2(neg_abs)) * LOG2E
    log_sig = w - drop
    if visible is not None:
        drop = jnp.where(visible, drop, 0.0)
    return log_sig, drop


def _sb_weights(log_sig, drop, visible, u, newer):
    a = jnp.exp2(log_sig - (_suffix_sum(drop, u) + newer))
    if visible is not None:
        a = jnp.where(visible, a, 0.0)
    return a


def _sb_block(w, visible, u, rsum):
    log_sig, drop = _sb_logs(w, visible)
    return _sb_weights(log_sig, drop, visible, u, rsum), rsum + jnp.sum(drop, axis=-1, keepdims=True)


def _sb_prompt_kernel(bias_ref, q_ref, k_ref, v_ref, u_ref, o_ref, *, tq, tk, scale):
    h = pl.program_id(1)
    qi = pl.program_id(2)
    bias2 = bias_ref[h] * LOG2E
    q = q_ref[...].astype(BF16)
    u = u_ref[...]
    nd = tq // tk

    def block(kb, visible, rsum, acc):
        start = pl.multiple_of(kb * tk, tk)
        k = k_ref[pl.ds(start, tk), :].astype(BF16)
        v = v_ref[pl.ds(start, tk), :].astype(BF16)
        w = _dot_nt(q, k) * (scale * LOG2E) + bias2
        a, rsum = _sb_block(w, visible, u, rsum)
        return rsum, acc + _dot(a.astype(BF16), v)

    row = lax.broadcasted_iota(jnp.int32, (tq, tk), 0)
    col = lax.broadcasted_iota(jnp.int32, (tq, tk), 1)
    carry = (jnp.zeros((tq, 1), F32), jnp.zeros((tq, HEAD_DIM), F32))
    for d in reversed(range(nd)):
        carry = block(qi * nd + d, col + d * tk < row, *carry)

    def body(it, carry):
        for d in range(nd):
            carry = block((qi - it) * nd - 1 - d, None, *carry)
        return carry

    rsum, acc = lax.fori_loop(0, qi, body, carry)
    o_ref[...] = acc.astype(o_ref.dtype)


def sb_attention_prompt(q, k, v, bias, *, B, T, tq=512, tk=256):
    N, HD = q.shape
    H = HD // HEAD_DIM
    tq = min(tq, T)
    tk = min(tk, tq)
    nq = T // tq
    u = jnp.asarray(np.triu(np.ones((tk, tk), np.float32), 0).T - np.eye(tk, dtype=np.float32), BF16)
    kern = functools.partial(_sb_prompt_kernel, tq=tq, tk=tk, scale=HEAD_DIM ** -0.5)
    return pl.pallas_call(
        kern,
        grid_spec=pltpu.PrefetchScalarGridSpec(
            num_scalar_prefetch=1,
            grid=(B, H, nq),
            in_specs=[pl.BlockSpec((tq, HEAD_DIM), lambda b, h, i, bias: (b * nq + i, h)),
                      pl.BlockSpec((T, HEAD_DIM), lambda b, h, i, bias: (b, h)),
                      pl.BlockSpec((T, HEAD_DIM), lambda b, h, i, bias: (b, h)),
                      pl.BlockSpec((tk, tk), lambda b, h, i, bias: (0, 0))],
            out_specs=pl.BlockSpec((tq, HEAD_DIM), lambda b, h, i, bias: (b * nq + i, h)),
        ),
        out_shape=jax.ShapeDtypeStruct((N, HD), BF16),
        compiler_params=_cparams(("arbitrary", "arbitrary", "arbitrary")),
        cost_estimate=pl.CostEstimate(flops=B * H * T * T * (2 * HEAD_DIM + 2 * tk), transcendentals=3 * B * H * T * T // 2,
                                      bytes_accessed=N * HD * (2 * 4 + 4 + 2)),
        name="sb_attn_prompt",
    )(bias.astype(F32), q, k, v, u)


def _sb_sample_kernel(pt_ref, q_ref, bias_ref, u_ref, new_k, new_v, *rest, c1, G, tq):
    pages = rest[:2 * G]
    o_ref, rsum_ref, acc_ref = rest[2 * G:]
    j = pl.program_id(1)
    R = q_ref.shape[0]
    H = R // tq
    n_tiles = new_k.shape[0] // LANES
    keys_per_tile = LANES // H

    @pl.when(j == 0)
    def _():
        rsum_ref[...] = jnp.zeros_like(rsum_ref)
        acc_ref[...] = jnp.zeros_like(acc_ref)

    row = lax.broadcasted_iota(jnp.int32, (R, LANES), 0)
    lane = lax.broadcasted_iota(jnp.int32, (R, LANES), 1)
    own_head = (lane % H) == (row // tq)

    def stack(tiles):
        return jnp.concatenate(tiles, axis=0)

    def page(k_ref, v_ref, causal, rsum):
        w = _dot_nt(q_ref[...], k_ref[...].astype(BF16)) * c1 + bias_ref[...]
        ws = stack([w[:, t * LANES:(t + 1) * LANES] for t in range(n_tiles)])
        if causal:
            visible = stack([jnp.logical_and(own_head, t * keys_per_tile + lane // H < row % tq)
                             for t in range(n_tiles)])
        else:
            visible = stack([own_head] * n_tiles)
        log_sig, drop = _sb_logs(ws, visible)
        tile_sum = jnp.sum(drop, axis=-1, keepdims=True)
        newer = [None] * n_tiles
        for t in reversed(range(n_tiles)):
            newer[t] = rsum
            rsum = rsum + tile_sum[t * R:(t + 1) * R]
        a = _sb_weights(log_sig, drop, visible, u_ref[...], stack(newer)).astype(BF16)
        a = jnp.concatenate([a[t * R:(t + 1) * R] for t in range(n_tiles)], axis=1)
        return rsum, _dot(a, v_ref[...].astype(BF16))

    def run(blocks):
        rsum = rsum_ref[...]
        acc = acc_ref[...]
        for k_ref, v_ref, causal in blocks:
            rsum, o = page(k_ref, v_ref, causal, rsum)
            acc = acc + o
        rsum_ref[...] = rsum
        acc_ref[...] = acc

    @pl.when(j == 0)
    def _():
        run([(new_k, new_v, True)])

    @pl.when(j > 0)
    def _():
        run([(pages[2 * g], pages[2 * g + 1], False) for g in range(G)])

    @pl.when(j == pl.num_programs(1) - 1)
    def _():
        o_ref[...] = acc_ref[...]


def sb_attention_sample(q, k_new, v_new, cache_k, cache_v, page_table, bias, *, B, T):
    HD = q.shape[1]
    H = HD // HEAD_DIM
    n_pages = page_table.shape[1]
    P = cache_k.shape[1]
    R = H * T
    assert LANES % H == 0 and n_pages % SAMPLE_PAGES_PER_STEP == 0
    qh = q.reshape(B, T, H, HEAD_DIM).transpose(0, 2, 1, 3).reshape(B, R, HEAD_DIM)
    bias_rows = (jnp.repeat(bias.astype(F32), T) * LOG2E).reshape(R, 1)
    u = jnp.asarray(np.triu(np.ones((LANES, LANES), np.float32), 0).T - np.eye(LANES, dtype=np.float32), BF16)
    G = SAMPLE_PAGES_PER_STEP
    page_view = lambda a: a.reshape(a.shape[0], P * H, HEAD_DIM)
    new_view = lambda a: page_view(jnp.pad(a.reshape(B, T, H, HEAD_DIM), ((0, 0), (0, P - T), (0, 0), (0, 0))))
    page = (None, P * H, HEAD_DIM)

    def page_spec(g):
        return pl.BlockSpec(page, lambda b, j, pt: (pt[b, n_pages - 1 - (jnp.maximum(j, 1) - 1) * G - g], 0, 0))

    in_specs = [pl.BlockSpec((None, R, HEAD_DIM), lambda b, j, pt: (b, 0, 0)),
                pl.BlockSpec((R, 1), lambda b, j, pt: (0, 0)),
                pl.BlockSpec((LANES, LANES), lambda b, j, pt: (0, 0)),
                pl.BlockSpec(page, lambda b, j, pt: (b, 0, 0)),
                pl.BlockSpec(page, lambda b, j, pt: (b, 0, 0))]
    args = [page_table, qh.astype(BF16), bias_rows, u, new_view(k_new), new_view(v_new)]
    ck, cv = page_view(cache_k), page_view(cache_v)
    for g in range(G):
        in_specs += [page_spec(g), page_spec(g)]
        args += [ck, cv]
    o = pl.pallas_call(
        functools.partial(_sb_sample_kernel, c1=HEAD_DIM ** -0.5 * LOG2E, G=G, tq=T),
        grid_spec=pltpu.PrefetchScalarGridSpec(
            num_scalar_prefetch=1,
            grid=(B, n_pages // G + 1),
            in_specs=in_specs,
            out_specs=pl.BlockSpec((None, R, HEAD_DIM), lambda b, j, pt: (b, 0, 0)),
            scratch_shapes=[pltpu.VMEM((R, 1), F32), pltpu.VMEM((R, HEAD_DIM), F32)],
        ),
        out_shape=jax.ShapeDtypeStruct((B, R, HEAD_DIM), F32),
        compiler_params=_cparams(("arbitrary", "arbitrary")),
        cost_estimate=pl.CostEstimate(flops=B * (n_pages + 1) * P * H * R * (4 * HEAD_DIM + 4 * LANES),
                                      transcendentals=3 * B * (n_pages + 1) * P * H * R,
                                      bytes_accessed=2 * B * (n_pages + 1) * P * HD * 4),
        name="sb_attn_sample",
    )(*args)
    return o.reshape(B, H, T, HEAD_DIM).transpose(0, 2, 1, 3).reshape(B * T, HD).astype(BF16)


def _experts_kernel(be_ref, nb_ref, first_ref, slot_ref, next_ref, x_ref, wg_hbm, wu_hbm, wd_hbm, o_ref,
                    wg_buf, wu_buf, wd_buf, wgb_ref, wub_ref, wdb_ref, sem, *, layer):
    i = pl.program_id(0)

    def weight_copies(e, s):
        return (pltpu.make_async_copy(wg_hbm.at[layer, e], wg_buf.at[s], sem.at[0, s]),
                pltpu.make_async_copy(wu_hbm.at[layer, e], wu_buf.at[s], sem.at[1, s]),
                pltpu.make_async_copy(wd_hbm.at[layer, e], wd_buf.at[s], sem.at[2, s]))

    @pl.when(i == 0)
    def _():
        for c in weight_copies(be_ref[0], 0):
            c.start()

    @pl.when(first_ref[i] == 1)
    def _():
        s = slot_ref[i]
        for c in weight_copies(be_ref[i], s):
            c.wait()

        @pl.when(next_ref[i] >= 0)
        def _():
            for c in weight_copies(next_ref[i], 1 - s):
                c.start()

        wgb_ref[...] = wg_buf[s].astype(BF16)
        wub_ref[...] = wu_buf[s].astype(BF16)
        wdb_ref[...] = wd_buf[s].astype(BF16)

    @pl.when(i < nb_ref[0])
    def _():
        x = x_ref[...]
        g = _dot(x, wgb_ref[...])
        up = _dot(x, wub_ref[...])
        a = (g * _sigmoid(g) * up).astype(BF16)
        o_ref[...] = _dot(a, wdb_ref[...]).astype(o_ref.dtype)

    @pl.when(i >= nb_ref[0])
    def _():
        o_ref[...] = jnp.zeros_like(o_ref)


def experts(x, block_e, n_used, wg, wu, wd, layer, tm, out_dtype, name):
    NR, D = x.shape
    Fd = wg.shape[-1]
    n_blocks = NR // tm
    pos = jnp.arange(n_blocks, dtype=jnp.int32)
    prev_e = jnp.concatenate([block_e[:1], block_e[:-1]])
    first = jnp.logical_and(pos < n_used[0], jnp.logical_or(pos == 0, block_e != prev_e))
    slot = (jnp.cumsum(first.astype(jnp.int32)) - 1) % 2
    first_pos = jnp.where(first, pos, n_blocks)
    next_first = lax.cummin(jnp.concatenate([first_pos[1:], jnp.full((1,), n_blocks, jnp.int32)]), reverse=True)
    next_e = jnp.where(next_first < n_blocks, block_e[jnp.minimum(next_first, n_blocks - 1)], -1)
    hbm = pl.BlockSpec(memory_space=pl.ANY)
    idx = lambda i, *_: (i, 0)
    return pl.pallas_call(
        functools.partial(_experts_kernel, layer=layer),
        grid_spec=pltpu.PrefetchScalarGridSpec(
            num_scalar_prefetch=5,
            grid=(n_blocks,),
            in_specs=[pl.BlockSpec((tm, D), idx), hbm, hbm, hbm],
            out_specs=pl.BlockSpec((tm, D), idx),
            scratch_shapes=[pltpu.VMEM((2, D, Fd), F32), pltpu.VMEM((2, D, Fd), F32), pltpu.VMEM((2, Fd, D), F32),
                            pltpu.VMEM((D, Fd), BF16), pltpu.VMEM((D, Fd), BF16), pltpu.VMEM((Fd, D), BF16),
                            pltpu.SemaphoreType.DMA((3, 2))],
        ),
        out_shape=jax.ShapeDtypeStruct((NR, D), out_dtype),
        compiler_params=_cparams(("arbitrary",)),
        cost_estimate=pl.CostEstimate(flops=6 * NR * D * Fd, transcendentals=NR * Fd,
                                      bytes_accessed=min(n_blocks, wg.shape[1]) * 12 * D * Fd + 4 * NR * D),
        name=name,
    )(block_e, n_used, first.astype(jnp.int32), slot.astype(jnp.int32), next_e.astype(jnp.int32), x, wg, wu, wd)


def _combine_kernel(x_ref, gate_ref, sh_ref, wt_ref, g_ref, o_ref):
    wts = wt_ref[...]
    y = sh_ref[...]
    for k in range(TOP_K):
        y = y + wts[:, k:k + 1] * g_ref[k].astype(F32)
    o_ref[...] = x_ref[...] + gate_ref[...] * y


def combine(x, gate, shared, wts, gathered, rows_per_group, tm, row0=0):
    N, D = x.shape
    tm = min(tm, N)
    assert row0 % tm == 0
    blk0 = row0 // tm
    if rows_per_group % tm == 0:
        bpg = rows_per_group // tm
        gate = gate.reshape(-1, 1, D)
        gate_spec = pl.BlockSpec((None, 1, D), lambda i: (i // bpg, 0, 0))
    else:
        gate = jnp.repeat(gate, rows_per_group, axis=0)
        gate_spec = pl.BlockSpec((tm, D), lambda i: (i, 0))
    row = pl.BlockSpec((tm, D), lambda i: (i, 0))
    return pl.pallas_call(
        _combine_kernel,
        grid=(N // tm,),
        in_specs=[row, gate_spec, row, pl.BlockSpec((tm, TOP_K), lambda i: (i, 0)),
                  pl.BlockSpec((TOP_K, tm, D), lambda i: (0, i + blk0, 0))],
        out_specs=row,
        out_shape=jax.ShapeDtypeStruct((N, D), F32),
        compiler_params=_cparams(("arbitrary",)),
        name="moe_combine",
    )(x, gate, shared, wts, gathered)


def moe_dispatch(h, idx, tm):
    N, D = h.shape
    NK = N * TOP_K
    n_blocks = -(-(NK + N_EXPERTS * (tm - 1)) // tm)
    member = jnp.sum(idx[:, :, None] == jnp.arange(N_EXPERTS)[None, None, :], axis=1).astype(jnp.int32)
    incl = jnp.cumsum(member, axis=0)
    counts = incl[-1]
    rank = jnp.take_along_axis(incl - member, idx, axis=1)
    padded = (counts + tm - 1) // tm * tm
    pend = jnp.cumsum(padded)
    pstart = pend - padded
    dest = pstart[idx] + rank
    tok = jnp.broadcast_to(jnp.arange(N, dtype=jnp.int32)[:, None], (N, TOP_K))
    rows = (jnp.arange(n_blocks * tm, dtype=jnp.int32) % N).at[dest.reshape(-1)].set(tok.reshape(-1))
    block_start = jnp.arange(n_blocks, dtype=jnp.int32) * tm
    block_e = jnp.minimum(jnp.sum(pend[None, :] <= block_start[:, None], axis=1), N_EXPERTS - 1).astype(jnp.int32)
    n_used = (pend[-1] // tm).astype(jnp.int32).reshape(1)
    xg = jnp.pad(h, ((0, N), (0, 0)))[rows]
    return xg, block_e, n_used, dest


def moe_ffn_joint(requests, layer, p):
    h_all = jnp.concatenate([r["h"] for r in requests], axis=0)
    idx_all = jnp.concatenate([r["idx"] for r in requests], axis=0)
    n_all, D = h_all.shape
    xg, block_e, n_used, dest = moe_dispatch(h_all, idx_all, ROUTED_ROWS)
    yb = experts(xg, block_e, n_used, p["exp_w_gate"], p["exp_w_up"], p["exp_w_down"], layer, ROUTED_ROWS, BF16,
                 "experts_routed")
    gathered = yb[dest.T.reshape(-1)].reshape(TOP_K, n_all, D)
    out, row0 = [], 0
    for r in requests:
        n = r["h"].shape[0]
        nsb = n // r["tm_shared"]
        shared = experts(r["h"], jnp.zeros((nsb,), jnp.int32), jnp.full((1,), nsb, jnp.int32),
                         p["sh_w_gate"][:, None], p["sh_w_up"][:, None], p["sh_w_down"][:, None], layer,
                         r["tm_shared"], F32, "experts_shared")
        out.append(combine(r["x"], r["gate"], shared, r["wts"], gathered, r["T"], COMBINE_ROWS, row0=row0))
        row0 += n
    return out


def trunk(x, mods, kv_mod, s0, past, p, cfg):
    B, T, D = x.shape
    N = B * T
    x = x.reshape(N, D)
    tm = cfg["tm"]
    HD = p["hg_w_out"].shape[1]
    lower = jnp.cumsum(jax.nn.softmax(p["hg_lb"].astype(F32), axis=0), axis=0)

    def split6(m):
        return [m[:, i * D:(i + 1) * D] for i in range(6)]

    sh_m, sc_m, g_m, sh_f, sc_f, g_f = split6(mods[0])
    h = norm_mod(x, p["norm_mix"][0], sh_m, sc_m, T, tm)
    mt = dict(tm=cfg["tm_mm"], tn=cfg["tn_mm"])
    qfig = mm(h, p["hg_w_in"], w_layer=0, name="hg_in", **mt)
    Tp = cfg["hg_t_pad"]
    if Tp != T:
        qfig = jnp.pad(qfig.reshape(B, T, -1), ((0, 0), (0, Tp - T), (0, 0))).reshape(B * Tp, -1)
    o, s_new = hgrn2(qfig, lower[0], p["hg_norm"][0], s0, B=B, T=Tp, C=cfg["hg_chunk"], tb=cfg["hg_tb"],
                     t_valid=None if Tp == T else T)
    if Tp != T:
        o = o.reshape(B, Tp, HD)[:, :T].reshape(N, HD)
    x = mm(o, p["hg_w_out"], w_layer=0, epilogue="resid", resid=x, gate=g_m, rows_per_group=T,
           name="hg_out", **mt)
    h, idx, wts = norm_mod(x, p["norm_ffn"][0], sh_f, sc_f, T, tm, router_w=p["router_w"],
                           router_bias=p["router_bias"], router_layer=0)
    x = yield dict(x=x, gate=g_f, T=T, h=h, idx=idx, wts=wts, tm_shared=cfg["tm_shared"])

    sh_kv, sc_kv = kv_mod[:, :D], kv_mod[:, D:]
    hk = norm_mod(x, p["kv_norm"], sh_kv, sc_kv, T, tm)
    k_new = mm(hk, p["w_kv"], n_out=HD, epilogue="headnorm", head_gain=p["k_norm"], name="k_proj", **mt)
    v_new = mm(hk, p["w_kv"], w_col0=HD, n_out=HD, name="v_proj", **mt)

    sh_m, sc_m, g_m, sh_f, sc_f, g_f = split6(mods[1])
    h = norm_mod(x, p["norm_mix"][1], sh_m, sc_m, T, tm)
    q = mm(h, p["sb_w_q"], w_layer=0, epilogue="headnorm", head_gain=p["q_norm"][0], name="q_proj", **mt)
    bias = p["sb_logit_bias"][0]
    if past is None:
        o = sb_attention_prompt(q, k_new, v_new, bias, B=B, T=T)
    else:
        o = sb_attention_sample(q, k_new, v_new, past[0], past[1], past[2], bias, B=B, T=T)
    x = mm(o, p["sb_w_out"], w_layer=0, epilogue="resid", resid=x, gate=g_m, rows_per_group=T,
           name="sb_out", **mt)
    h, idx, wts = norm_mod(x, p["norm_ffn"][1], sh_f, sc_f, T, tm, router_w=p["router_w"],
                           router_bias=p["router_bias"], router_layer=1)
    x = yield dict(x=x, gate=g_f, T=T, h=h, idx=idx, wts=wts, tm_shared=cfg["tm_shared"])
    H = HD // HEAD_DIM
    return (x.reshape(B, T, D), s_new[None], k_new.reshape(B, T, H, HEAD_DIM), v_new.reshape(B, T, H, HEAD_DIM))


PROMPT_CFG = dict(tm=512, tm_mm=1024, tn_mm=1024, hg_chunk=HG_CHUNK, hg_tb=512, hg_t_pad=None,
                  tm_shared=512)
SAMPLE_CFG = dict(tm=32, tm_mm=32, tn_mm=1024, hg_chunk=HG_CHUNK, hg_tb=HG_CHUNK, hg_t_pad=HG_CHUNK,
                  tm_shared=32)
MOD_ROWS = 16


def kernel(x_prompt, x_sample, c_prompt, c_sample, state_hgrn, cache_k, cache_v, page_table, ada_w, ada_b,
           norm_mix, norm_ffn, hg_w_in, hg_lb, hg_norm, hg_w_out, kv_ada_w, kv_ada_b, kv_norm, w_kv, k_norm,
           sb_w_q, q_norm, sb_logit_bias, sb_w_out, router_w, router_bias, exp_w_gate, exp_w_up, exp_w_down,
           sh_w_gate, sh_w_up, sh_w_down):
    p = dict(ada_w=ada_w, ada_b=ada_b, norm_mix=norm_mix, norm_ffn=norm_ffn, hg_w_in=hg_w_in, hg_lb=hg_lb,
             hg_norm=hg_norm, hg_w_out=hg_w_out, kv_ada_w=kv_ada_w, kv_ada_b=kv_ada_b, kv_norm=kv_norm,
             w_kv=w_kv, k_norm=k_norm, sb_w_q=sb_w_q, q_norm=q_norm, sb_logit_bias=sb_logit_bias,
             sb_w_out=sb_w_out, router_w=router_w, router_bias=router_bias, exp_w_gate=exp_w_gate,
             exp_w_up=exp_w_up, exp_w_down=exp_w_down, sh_w_gate=sh_w_gate, sh_w_up=sh_w_up,
             sh_w_down=sh_w_down)
    Bp, Tp, _ = x_prompt.shape
    Bs = x_sample.shape[0]
    H = hg_w_out.shape[1] // HEAD_DIM
    c_all = jnp.concatenate([c_prompt, c_sample], axis=0)
    c_all = jnp.pad(c_all, ((0, MOD_ROWS - c_all.shape[0]), (0, 0)))
    mods = [mm(c_all, ada_w, w_layer=l, bias=ada_b[l], silu_in=True, tn=1024, name="ada") for l in range(2)]
    kv_mod = mm(c_all, kv_ada_w, bias=kv_ada_b, silu_in=True, tn=1024, name="ada_kv")
    rows_p = slice(0, Bp)
    rows_s = slice(Bp, Bp + Bs)
    cfg_p = dict(PROMPT_CFG, hg_t_pad=Tp)
    s0_prompt = jnp.zeros((Bp, H, HEAD_DIM, HEAD_DIM), state_hgrn.dtype)
    groups = [trunk(x_prompt, [m[rows_p] for m in mods], kv_mod[rows_p], s0_prompt, None, p, cfg_p),
              trunk(x_sample, [m[rows_s] for m in mods], kv_mod[rows_s], state_hgrn[0],
                    (cache_k, cache_v, page_table), p, SAMPLE_CFG)]
    requests = [next(g) for g in groups]
    results = [None, None]
    layer = 0
    while any(r is None for r in results):
        new_x = moe_ffn_joint(requests, layer, p)
        for g, gen in enumerate(groups):
            try:
                requests[g] = gen.send(new_x[g])
            except StopIteration as done:
                results[g] = done.value
        layer += 1
    (y_p, s_p, k_p, v_p), (y_s, s_s, k_s, v_s) = results
    return (y_p, y_s, s_p, s_s, k_p, v_p, k_s, v_s)
```

```python
import functools
import math

import numpy as np
import jax
import jax.numpy as jnp
from jax import lax
from jax.experimental import pallas as pl
from jax.experimental.pallas import tpu as pltpu

F32 = jnp.float32
BF16 = jnp.bfloat16

LANES = 128
SAMPLE_PAGES_PER_STEP = 4
VMEM_LIMIT_BYTES = 52 * 1024 * 1024
NORM_EPS = 1e-6
HEAD_DIM = 128
N_EXPERTS = 64
TOP_K = 8
ROUTED_SCALE = 2.5
HG_CHUNK = 64
COMBINE_ROWS = 256
ROUTED_ROWS = 512


def _cparams(sem):
    return pltpu.CompilerParams(dimension_semantics=sem, vmem_limit_bytes=VMEM_LIMIT_BYTES)


def _sigmoid(x):
    return 1.0 / (1.0 + jnp.exp(-x))


def _dot(a, b):
    return jnp.dot(a, b, preferred_element_type=F32)


def _dot_nt(a, b):
    return lax.dot_general(a, b, (((1,), (1,)), ((), ())), preferred_element_type=F32)


def _dot_tn(a, b):
    return lax.dot_general(a, b, (((0,), (0,)), ((), ())), preferred_element_type=F32)


def _norm_mod_kernel(x_ref, g_ref, sh_ref, sc_ref, *rest, with_router):
    x = x_ref[...]
    ms = jnp.mean(x * x, axis=-1, keepdims=True)
    y = x * lax.rsqrt(ms + NORM_EPS) * g_ref[...]
    h = y * (1.0 + sc_ref[...]) + sh_ref[...]
    if with_router:
        rw_ref, rb_ref, o_ref, idx_ref, wt_ref = rest
        logits = jnp.dot(h, rw_ref[...], preferred_element_type=F32, precision=lax.Precision.HIGHEST)
        scores = _sigmoid(logits)
        sel = scores + rb_ref[...]
        tm, E = sel.shape
        lane = lax.broadcasted_iota(jnp.int32, (tm, E), 1)
        slot = lax.broadcasted_iota(jnp.int32, (tm, TOP_K), 1)
        idx = jnp.zeros((tm, TOP_K), jnp.int32)
        wts = jnp.zeros((tm, TOP_K), F32)
        for k in range(TOP_K):
            best = jnp.max(sel, axis=-1, keepdims=True)
            pick = jnp.min(jnp.where(sel == best, lane, E), axis=-1, keepdims=True)
            hit = lane == pick
            w_k = jnp.sum(jnp.where(hit, scores, 0.0), axis=-1, keepdims=True)
            sel = jnp.where(hit, -jnp.inf, sel)
            idx = jnp.where(slot == k, pick, idx)
            wts = jnp.where(slot == k, w_k, wts)
        idx_ref[...] = idx
        wt_ref[...] = wts / jnp.sum(wts, axis=-1, keepdims=True) * ROUTED_SCALE
    else:
        o_ref, = rest
    o_ref[...] = h.astype(o_ref.dtype)


def norm_mod(x, gain, shift, scale, rows_per_group, tm, router_w=None, router_bias=None, router_layer=None):
    N, D = x.shape
    tm = min(tm, N)
    with_router = router_w is not None
    if rows_per_group % tm == 0:
        bpg = rows_per_group // tm
        sh = shift.reshape(-1, 1, D)
        sc = scale.reshape(-1, 1, D)
        mod_spec = pl.BlockSpec((None, 1, D), lambda i: (i // bpg, 0, 0))
    else:
        sh = jnp.repeat(shift, rows_per_group, axis=0)
        sc = jnp.repeat(scale, rows_per_group, axis=0)
        mod_spec = pl.BlockSpec((tm, D), lambda i: (i, 0))
    args = [x, gain.reshape(1, D), sh, sc]
    in_specs = [pl.BlockSpec((tm, D), lambda i: (i, 0)),
                pl.BlockSpec((1, D), lambda i: (0, 0)),
                mod_spec, mod_spec]
    out_specs = pl.BlockSpec((tm, D), lambda i: (i, 0))
    out_shape = jax.ShapeDtypeStruct((N, D), BF16)
    if with_router:
        E = router_w.shape[-1]
        args += [router_w, router_bias.astype(F32).reshape(-1, 1, E)]
        in_specs += [pl.BlockSpec((None, D, E), lambda i: (router_layer, 0, 0)),
                     pl.BlockSpec((None, 1, E), lambda i: (router_layer, 0, 0))]
        out_specs = [out_specs, pl.BlockSpec((tm, TOP_K), lambda i: (i, 0)),
                     pl.BlockSpec((tm, TOP_K), lambda i: (i, 0))]
        out_shape = [out_shape, jax.ShapeDtypeStruct((N, TOP_K), jnp.int32),
                     jax.ShapeDtypeStruct((N, TOP_K), F32)]
    return pl.pallas_call(
        functools.partial(_norm_mod_kernel, with_router=with_router),
        grid=(N // tm,),
        in_specs=in_specs,
        out_specs=out_specs,
        out_shape=out_shape,
        compiler_params=_cparams(("arbitrary",)),
        name="norm_router" if with_router else "norm_mod",
    )(*args)


def _mm_kernel(*refs, silu_in, has_bias, epilogue, tn):
    it = iter(refs)
    x_ref = next(it)
    w_ref = next(it)
    b_ref = next(it) if has_bias else None
    if epilogue == "headnorm":
        hg_ref = next(it)
    elif epilogue == "resid":
        r_ref = next(it)
        gate_ref = next(it)
    o_ref = next(it)
    wbf_ref = next(it)

    @pl.when(pl.program_id(1) == 0)
    def _():
        wbf_ref[...] = w_ref[...].astype(BF16)

    x = x_ref[...]
    if silu_in:
        x = x.astype(F32)
        x = x * _sigmoid(x)
    y = _dot(x.astype(BF16), wbf_ref[...])
    if has_bias:
        y = y + b_ref[...]
    if epilogue == "headnorm":
        gain = hg_ref[...]
        for c in range(tn // HEAD_DIM):
            sl = slice(c * HEAD_DIM, (c + 1) * HEAD_DIM)
            yc = y[:, sl]
            ms = jnp.mean(yc * yc, axis=-1, keepdims=True)
            o_ref[:, sl] = (yc * lax.rsqrt(ms + NORM_EPS) * gain).astype(o_ref.dtype)
    elif epilogue == "resid":
        o_ref[...] = (r_ref[...] + gate_ref[...] * y).astype(o_ref.dtype)
    else:
        o_ref[...] = y.astype(o_ref.dtype)


def mm(x, w, *, w_layer=None, w_col0=0, n_out=None, bias=None, silu_in=False, out_dtype=F32,
       tm=512, tn=512, epilogue=None, head_gain=None, resid=None, gate=None, rows_per_group=None,
       name="mm"):
    M, K = x.shape
    Ntot = w.shape[-1]
    N = Ntot - w_col0 if n_out is None else n_out
    tm = min(tm, M)
    tn = min(tn, N)
    assert M % tm == 0 and N % tn == 0 and w_col0 % tn == 0
    c0 = w_col0 // tn
    if w.ndim == 3:
        w_spec = pl.BlockSpec((None, K, tn), lambda j, i: (w_layer, 0, j + c0))
    else:
        w_spec = pl.BlockSpec((K, tn), lambda j, i: (0, j + c0))
    args = [x, w]
    in_specs = [pl.BlockSpec((tm, K), lambda j, i: (i, 0)), w_spec]
    if bias is not None:
        args.append(bias.reshape(1, -1))
        in_specs.append(pl.BlockSpec((1, tn), lambda j, i: (0, j + c0)))
    if epilogue == "headnorm":
        args.append(head_gain.reshape(1, HEAD_DIM))
        in_specs.append(pl.BlockSpec((1, HEAD_DIM), lambda j, i: (0, 0)))
    elif epilogue == "resid":
        args.append(resid)
        in_specs.append(pl.BlockSpec((tm, tn), lambda j, i: (i, j)))
        if rows_per_group % tm == 0:
            bpg = rows_per_group // tm
            args.append(gate.reshape(-1, 1, N))
            in_specs.append(pl.BlockSpec((None, 1, tn), lambda j, i: (i // bpg, 0, j)))
        else:
            args.append(jnp.repeat(gate, rows_per_group, axis=0))
            in_specs.append(pl.BlockSpec((tm, tn), lambda j, i: (i, j)))
    kern = functools.partial(_mm_kernel, silu_in=silu_in, has_bias=bias is not None,
                             epilogue=epilogue, tn=tn)
    return pl.pallas_call(
        kern,
        grid=(N // tn, M // tm),
        in_specs=in_specs,
        out_specs=pl.BlockSpec((tm, tn), lambda j, i: (i, j)),
        out_shape=jax.ShapeDtypeStruct((M, N), out_dtype),
        scratch_shapes=[pltpu.VMEM((K, tn), BF16)],
        compiler_params=_cparams(("arbitrary", "arbitrary")),
        cost_estimate=pl.CostEstimate(flops=2 * M * K * N, transcendentals=0,
                                      bytes_accessed=(N // tn) * M * K * x.dtype.itemsize + 4 * K * N + 4 * M * N),
        name=name,
    )(*args)


def _hgrn_constants(C):
    r = np.arange(C)
    mats = [(r[None, :] <= r[:, None]).astype(np.float32)]
    masks = [np.eye(C, dtype=np.float32)]
    m = C // 2
    while m >= 1:
        base = (r // (2 * m)) * (2 * m)
        pivot = base + m - 1
        upper = r > pivot
        j = r[None, :]
        mat = np.where(upper[:, None], (j > pivot[:, None]) & (j <= r[:, None]),
                       (j > r[:, None]) & (j <= pivot[:, None]))
        mats.append(mat.astype(np.float32))
        same = base[:, None] == base[None, :]
        masks.append((same & upper[:, None] & (~upper)[None, :]).astype(np.float32))
        m //= 2
    return np.concatenate(mats, axis=0), np.stack(masks, axis=0)


def _hgrn_kernel(q_ref, fu_ref, i_ref, g_ref, lb_ref, gain_ref, s0_ref, mat_ref, mask_ref,
                 o_ref, sout_ref, s_ref, *, C, n_chunks, n_levels, t_valid):
    t = pl.program_id(2)

    @pl.when(t == 0)
    def _():
        s_ref[...] = s0_ref[...].T

    lb = lb_ref[...]
    gain = gain_ref[...]
    staged = []
    for c in range(n_chunks):
        rows = slice(c * C, (c + 1) * C)
        q = q_ref[rows, :]
        f = lb + (1.0 - lb) * _sigmoid(fu_ref[rows, :])
        logf = jnp.log(f)
        kk = 1.0 - f
        if t_valid is not None:
            valid = (lax.broadcasted_iota(jnp.int32, (C, HEAD_DIM), 0) + c * C) < t_valid
            logf = jnp.where(valid, logf, 0.0)
            kk = jnp.where(valid, kk, 0.0)
        hi = logf.astype(BF16)
        lo = (logf - hi.astype(F32)).astype(BF16)
        e = _dot(mat_ref[...], hi) + _dot(mat_ref[...], lo)
        b = e[0:C, :]
        scores = mask_ref[0] * _dot_nt(q.astype(BF16), kk.astype(BF16))
        for l in range(1, n_levels + 1):
            x = jnp.exp(e[l * C:(l + 1) * C, :])
            scores = scores + mask_ref[l] * _dot_nt((q * x).astype(BF16), (kk * x).astype(BF16))
        b_end = b[C - 1:C, :]
        staged.append(((q * jnp.exp(b)).astype(BF16), scores.astype(BF16), i_ref[rows, :].astype(BF16),
                       (kk * jnp.exp(b_end - b)).astype(BF16), jnp.exp(b_end)))

    St = s_ref[...]
    for c, (q_dec, scores, vb, khat, decay) in enumerate(staged):
        rows = slice(c * C, (c + 1) * C)
        o = _dot_nt(q_dec, St.astype(BF16)) + _dot(scores, vb)
        St = decay * St + _dot_tn(vb, khat)
        ms = jnp.mean(o * o, axis=-1, keepdims=True)
        gt = g_ref[rows, :]
        o_ref[rows, :] = (o * lax.rsqrt(ms + NORM_EPS) * gain * (gt * _sigmoid(gt))).astype(o_ref.dtype)
    s_ref[...] = St

    @pl.when(t == pl.num_programs(2) - 1)
    def _():
        sout_ref[...] = s_ref[...].T


def hgrn2(qfig, lb, gain, s0, *, B, T, C, tb, t_valid=None):
    H = s0.shape[1]
    HD = H * HEAD_DIM
    x3 = qfig.reshape(B, T, 4 * HD)
    mats, masks = _hgrn_constants(C)
    n_levels = masks.shape[0] - 1
    kern = functools.partial(_hgrn_kernel, C=C, n_chunks=tb // C, n_levels=n_levels, t_valid=t_valid)

    def col(off):
        return pl.BlockSpec((None, tb, HEAD_DIM), lambda b, h, t: (b, t, h + off * H))

    head_vec = pl.BlockSpec((1, HEAD_DIM), lambda b, h, t: (0, h))
    o, s_out = pl.pallas_call(
        kern,
        grid=(B, H, T // tb),
        in_specs=[col(0), col(1), col(2), col(3), head_vec, head_vec,
                  pl.BlockSpec((None, None, HEAD_DIM, HEAD_DIM), lambda b, h, t: (b, h, 0, 0)),
                  pl.BlockSpec(mats.shape, lambda b, h, t: (0, 0)),
                  pl.BlockSpec(masks.shape, lambda b, h, t: (0, 0, 0))],
        out_specs=[pl.BlockSpec((None, tb, HEAD_DIM), lambda b, h, t: (b, t, h)),
                   pl.BlockSpec((None, None, HEAD_DIM, HEAD_DIM), lambda b, h, t: (b, h, 0, 0))],
        out_shape=[jax.ShapeDtypeStruct((B, T, HD), BF16),
                   jax.ShapeDtypeStruct(s0.shape, F32)],
        scratch_shapes=[pltpu.VMEM((HEAD_DIM, HEAD_DIM), F32)],
        compiler_params=_cparams(("arbitrary", "arbitrary", "arbitrary")),
        cost_estimate=pl.CostEstimate(flops=2 * B * T * H * HEAD_DIM * (3 * C * (n_levels + 1) + 4 * HEAD_DIM),
                                      transcendentals=B * T * HD * (n_levels + 8),
                                      bytes_accessed=B * T * HD * 18),
        name="hgrn2",
    )(x3, x3, x3, x3, lb.reshape(1, HD), gain.reshape(1, HD), s0,
      jnp.asarray(mats, BF16), jnp.asarray(masks))
    return o.reshape(B * T, HD), s_out


LOG2E = math.log2(math.e)
SIGN_BIT = 0x80000000


def _suffix_sum(x, u):
    hi = x.astype(BF16)
    lo = (x - hi.astype(F32)).astype(BF16)
    return _dot(hi, u) + _dot(lo, u)


def _sb_logs(w, visible):
    neg_abs = lax.bitcast_convert_type(lax.bitcast_convert_type(w, jnp.uint32) | jnp.uint32(SIGN_BIT), F32)
    drop = jnp.maximum(w, 0.0) + jnp.log(1.0 + jnp.exp2(neg_abs)) * LOG2E
    log_sig = w - drop
    if visible is not None:
        drop = jnp.where(visible, drop, 0.0)
    return log_sig, drop


def _sb_weights(log_sig, drop, visible, u, newer):
    a = jnp.exp2(log_sig - (_suffix_sum(drop, u) + newer))
    if visible is not None:
        a = jnp.where(visible, a, 0.0)
    return a


def _sb_block(w, visible, u, rsum):
    log_sig, drop = _sb_logs(w, visible)
    return _sb_weights(log_sig, drop, visible, u, rsum), rsum + jnp.sum(drop, axis=-1, keepdims=True)


def _sb_prompt_kernel(bias_ref, q_ref, k_ref, v_ref, u_ref, o_ref, *, tq, tk, scale):
    h = pl.program_id(1)
    qi = pl.program_id(2)
    bias2 = bias_ref[h] * LOG2E
    q = q_ref[...].astype(BF16)
    u = u_ref[...]
    nd = tq // tk

    def block(kb, visible, rsum, acc):
        start = pl.multiple_of(kb * tk, tk)
        k = k_ref[pl.ds(start, tk), :].astype(BF16)
        v = v_ref[pl.ds(start, tk), :].astype(BF16)
        w = _dot_nt(q, k) * (scale * LOG2E) + bias2
        a, rsum = _sb_block(w, visible, u, rsum)
        return rsum, acc + _dot(a.astype(BF16), v)

    row = lax.broadcasted_iota(jnp.int32, (tq, tk), 0)
    col = lax.broadcasted_iota(jnp.int32, (tq, tk), 1)
    carry = (jnp.zeros((tq, 1), F32), jnp.zeros((tq, HEAD_DIM), F32))
    for d in reversed(range(nd)):
        carry = block(qi * nd + d, col + d * tk < row, *carry)

    def body(it, carry):
        for d in range(nd):
            carry = block((qi - it) * nd - 1 - d, None, *carry)
        return carry

    rsum, acc = lax.fori_loop(0, qi, body, carry)
    o_ref[...] = acc.astype(o_ref.dtype)


def sb_attention_prompt(q, k, v, bias, *, B, T, tq=512, tk=256):
    N, HD = q.shape
    H = HD // HEAD_DIM
    tq = min(tq, T)
    tk = min(tk, tq)
    nq = T // tq
    u = jnp.asarray(np.triu(np.ones((tk, tk), np.float32), 0).T - np.eye(tk, dtype=np.float32), BF16)
    kern = functools.partial(_sb_prompt_kernel, tq=tq, tk=tk, scale=HEAD_DIM ** -0.5)
    return pl.pallas_call(
        kern,
        grid_spec=pltpu.PrefetchScalarGridSpec(
            num_scalar_prefetch=1,
            grid=(B, H, nq),
            in_specs=[pl.BlockSpec((tq, HEAD_DIM), lambda b, h, i, bias: (b * nq + i, h)),
                      pl.BlockSpec((T, HEAD_DIM), lambda b, h, i, bias: (b, h)),
                      pl.BlockSpec((T, HEAD_DIM), lambda b, h, i, bias: (b, h)),
                      pl.BlockSpec((tk, tk), lambda b, h, i, bias: (0, 0))],
            out_specs=pl.BlockSpec((tq, HEAD_DIM), lambda b, h, i, bias: (b * nq + i, h)),
        ),
        out_shape=jax.ShapeDtypeStruct((N, HD), BF16),
        compiler_params=_cparams(("arbitrary", "arbitrary", "arbitrary")),
        cost_estimate=pl.CostEstimate(flops=B * H * T * T * (2 * HEAD_DIM + 2 * tk), transcendentals=3 * B * H * T * T // 2,
                                      bytes_accessed=N * HD * (2 * 4 + 4 + 2)),
        name="sb_attn_prompt",
    )(bias.astype(F32), q, k, v, u)


def _sb_sample_kernel(pt_ref, q_ref, bias_ref, u_ref, new_k, new_v, *rest, c1, G, tq):
    pages = rest[:2 * G]
    o_ref, rsum_ref, acc_ref = rest[2 * G:]
    j = pl.program_id(1)
    R = q_ref.shape[0]
    H = R // tq
    n_tiles = new_k.shape[0] // LANES
    keys_per_tile = LANES // H

    @pl.when(j == 0)
    def _():
        rsum_ref[...] = jnp.zeros_like(rsum_ref)
        acc_ref[...] = jnp.zeros_like(acc_ref)

    row = lax.broadcasted_iota(jnp.int32, (R, LANES), 0)
    lane = lax.broadcasted_iota(jnp.int32, (R, LANES), 1)
    own_head = (lane % H) == (row // tq)

    def stack(tiles):
        return jnp.concatenate(tiles, axis=0)

    def page(k_ref, v_ref, causal, rsum):
        w = _dot_nt(q_ref[...], k_ref[...].astype(BF16)) * c1 + bias_ref[...]
        ws = stack([w[:, t * LANES:(t + 1) * LANES] for t in range(n_tiles)])
        if causal:
            visible = stack([jnp.logical_and(own_head, t * keys_per_tile + lane // H < row % tq)
                             for t in range(n_tiles)])
        else:
            visible = stack([own_head] * n_tiles)
        log_sig, drop = _sb_logs(ws, visible)
        tile_sum = jnp.sum(drop, axis=-1, keepdims=True)
        newer = [None] * n_tiles
        for t in reversed(range(n_tiles)):
            newer[t] = rsum
            rsum = rsum + tile_sum[t * R:(t + 1) * R]
        a = _sb_weights(log_sig, drop, visible, u_ref[...], stack(newer)).astype(BF16)
        a = jnp.concatenate([a[t * R:(t + 1) * R] for t in range(n_tiles)], axis=1)
        return rsum, _dot(a, v_ref[...].astype(BF16))

    def run(blocks):
        rsum = rsum_ref[...]
        acc = acc_ref[...]
        for k_ref, v_ref, causal in blocks:
            rsum, o = page(k_ref, v_ref, causal, rsum)
            acc = acc + o
        rsum_ref[...] = rsum
        acc_ref[...] = acc

    @pl.when(j == 0)
    def _():
        run([(new_k, new_v, True)])

    @pl.when(j > 0)
    def _():
        run([(pages[2 * g], pages[2 * g + 1], False) for g in range(G)])

    @pl.when(j == pl.num_programs(1) - 1)
    def _():
        o_ref[...] = acc_ref[...]


def sb_attention_sample(q, k_new, v_new, cache_k, cache_v, page_table, bias, *, B, T):
    HD = q.shape[1]
    H = HD // HEAD_DIM
    n_pages = page_table.shape[1]
    P = cache_k.shape[1]
    R = H * T
    assert LANES % H == 0 and n_pages % SAMPLE_PAGES_PER_STEP == 0
    qh = q.reshape(B, T, H, HEAD_DIM).transpose(0, 2, 1, 3).reshape(B, R, HEAD_DIM)
    bias_rows = (jnp.repeat(bias.astype(F32), T) * LOG2E).reshape(R, 1)
    u = jnp.asarray(np.triu(np.ones((LANES, LANES), np.float32), 0).T - np.eye(LANES, dtype=np.float32), BF16)
    G = SAMPLE_PAGES_PER_STEP
    page_view = lambda a: a.reshape(a.shape[0], P * H, HEAD_DIM)
    new_view = lambda a: page_view(jnp.pad(a.reshape(B, T, H, HEAD_DIM), ((0, 0), (0, P - T), (0, 0), (0, 0))))
    page = (None, P * H, HEAD_DIM)

    def page_spec(g):
        return pl.BlockSpec(page, lambda b, j, pt: (pt[b, n_pages - 1 - (jnp.maximum(j, 1) - 1) * G - g], 0, 0))

    in_specs = [pl.BlockSpec((None, R, HEAD_DIM), lambda b, j, pt: (b, 0, 0)),
                pl.BlockSpec((R, 1), lambda b, j, pt: (0, 0)),
                pl.BlockSpec((LANES, LANES), lambda b, j, pt: (0, 0)),
                pl.BlockSpec(page, lambda b, j, pt: (b, 0, 0)),
                pl.BlockSpec(page, lambda b, j, pt: (b, 0, 0))]
    args = [page_table, qh.astype(BF16), bias_rows, u, new_view(k_new), new_view(v_new)]
    ck, cv = page_view(cache_k), page_view(cache_v)
    for g in range(G):
        in_specs += [page_spec(g), page_spec(g)]
        args += [ck, cv]
    o = pl.pallas_call(
        functools.partial(_sb_sample_kernel, c1=HEAD_DIM ** -0.5 * LOG2E, G=G, tq=T),
        grid_spec=pltpu.PrefetchScalarGridSpec(
            num_scalar_prefetch=1,
            grid=(B, n_pages // G + 1),
            in_specs=in_specs,
            out_specs=pl.BlockSpec((None, R, HEAD_DIM), lambda b, j, pt: (b, 0, 0)),
            scratch_shapes=[pltpu.VMEM((R, 1), F32), pltpu.VMEM((R, HEAD_DIM), F32)],
        ),
        out_shape=jax.ShapeDtypeStruct((B, R, HEAD_DIM), F32),
        compiler_params=_cparams(("arbitrary", "arbitrary")),
        cost_estimate=pl.CostEstimate(flops=B * (n_pages + 1) * P * H * R * (4 * HEAD_DIM + 4 * LANES),
                                      transcendentals=3 * B * (n_pages + 1) * P * H * R,
                                      bytes_accessed=2 * B * (n_pages + 1) * P * HD * 4),
        name="sb_attn_sample",
    )(*args)
    return o.reshape(B, H, T, HEAD_DIM).transpose(0, 2, 1, 3).reshape(B * T, HD).astype(BF16)


def _experts_kernel(be_ref, nb_ref, first_ref, slot_ref, next_ref, x_ref, wg_hbm, wu_hbm, wd_hbm, *rest, layer,
                    has_prev):
    o_ref, wg_buf, wu_buf, wd_buf, wgb_ref, wub_ref, wdb_ref, sem = rest[1:] if has_prev else rest
    i = pl.program_id(0)

    def weight_copies(e, s):
        return (pltpu.make_async_copy(wg_hbm.at[layer, e], wg_buf.at[s], sem.at[0, s]),
                pltpu.make_async_copy(wu_hbm.at[layer, e], wu_buf.at[s], sem.at[1, s]),
                pltpu.make_async_copy(wd_hbm.at[layer, e], wd_buf.at[s], sem.at[2, s]))

    @pl.when(jnp.logical_and(i == 0, first_ref[0] == 1))
    def _():
        for c in weight_copies(be_ref[0], 0):
            c.start()

    @pl.when(first_ref[i] == 1)
    def _():
        s = slot_ref[i]
        for c in weight_copies(be_ref[i], s):
            c.wait()

        @pl.when(next_ref[i] >= 0)
        def _():
            for c in weight_copies(next_ref[i], 1 - s):
                c.start()

        wgb_ref[...] = wg_buf[s].astype(BF16)
        wub_ref[...] = wu_buf[s].astype(BF16)
        wdb_ref[...] = wd_buf[s].astype(BF16)

    @pl.when(i < nb_ref[0])
    def _():
        x = x_ref[...]
        g = _dot(x, wgb_ref[...])
        up = _dot(x, wub_ref[...])
        a = (g * _sigmoid(g) * up).astype(BF16)
        o_ref[...] = _dot(a, wdb_ref[...]).astype(o_ref.dtype)

    @pl.when(i >= nb_ref[0])
    def _():
        o_ref[...] = jnp.zeros_like(o_ref)


def experts(x, block_e, n_used, wg, wu, wd, layer, tm, out_dtype, name, out_rows=None, out_block0=0, prev=None):
    D = x.shape[1]
    n_x = x.shape[0] // tm
    Fd = wg.shape[-1]
    n_blocks = block_e.shape[0]
    NR = n_blocks * tm if out_rows is None else out_rows
    pos = jnp.arange(n_blocks, dtype=jnp.int32)
    prev_e = jnp.concatenate([block_e[:1], block_e[:-1]])
    first = jnp.logical_and(pos < n_used[0], jnp.logical_or(pos == 0, block_e != prev_e))
    slot = (jnp.cumsum(first.astype(jnp.int32)) - 1) % 2
    first_pos = jnp.where(first, pos, n_blocks)
    next_first = lax.cummin(jnp.concatenate([first_pos[1:], jnp.full((1,), n_blocks, jnp.int32)]), reverse=True)
    next_e = jnp.where(next_first < n_blocks, block_e[jnp.minimum(next_first, n_blocks - 1)], -1)
    hbm = pl.BlockSpec(memory_space=pl.ANY)
    args = [block_e, n_used, first.astype(jnp.int32), slot.astype(jnp.int32), next_e.astype(jnp.int32), x, wg, wu, wd]
    in_specs = [pl.BlockSpec((tm, D), lambda i, *_: (jnp.minimum(i, n_x - 1), 0)), hbm, hbm, hbm]
    aliases = {}
    if prev is not None:
        aliases = {len(args): 0}
        args.append(prev)
        in_specs.append(hbm)
    return pl.pallas_call(
        functools.partial(_experts_kernel, layer=layer, has_prev=prev is not None),
        grid_spec=pltpu.PrefetchScalarGridSpec(
            num_scalar_prefetch=5,
            grid=(n_blocks,),
            in_specs=in_specs,
            out_specs=pl.BlockSpec((tm, D), lambda i, *_: (i + out_block0, 0)),
            scratch_shapes=[pltpu.VMEM((2, D, Fd), F32), pltpu.VMEM((2, D, Fd), F32), pltpu.VMEM((2, Fd, D), F32),
                            pltpu.VMEM((D, Fd), BF16), pltpu.VMEM((D, Fd), BF16), pltpu.VMEM((Fd, D), BF16),
                            pltpu.SemaphoreType.DMA((3, 2))],
        ),
        out_shape=jax.ShapeDtypeStruct((NR, D), out_dtype),
        compiler_params=_cparams(("arbitrary",)),
        cost_estimate=pl.CostEstimate(flops=6 * n_x * tm * D * Fd, transcendentals=n_x * tm * Fd,
                                      bytes_accessed=min(n_x, wg.shape[1]) * 12 * D * Fd + 4 * n_x * tm * D),
        input_output_aliases=aliases,
        name=name,
    )(*args)


def _combine_kernel(x_ref, gate_ref, sh_ref, wt_ref, g_ref, o_ref):
    wts = wt_ref[...]
    y = sh_ref[...]
    for k in range(TOP_K):
        y = y + wts[:, k:k + 1] * g_ref[k].astype(F32)
    o_ref[...] = x_ref[...] + gate_ref[...] * y


def combine(x, gate, shared, wts, gathered, rows_per_group, tm, row0=0):
    N, D = x.shape
    tm = min(tm, N)
    assert row0 % tm == 0
    blk0 = row0 // tm
    if rows_per_group % tm == 0:
        bpg = rows_per_group // tm
        gate = gate.reshape(-1, 1, D)
        gate_spec = pl.BlockSpec((None, 1, D), lambda i: (i // bpg, 0, 0))
    else:
        gate = jnp.repeat(gate, rows_per_group, axis=0)
        gate_spec = pl.BlockSpec((tm, D), lambda i: (i, 0))
    row = pl.BlockSpec((tm, D), lambda i: (i, 0))
    return pl.pallas_call(
        _combine_kernel,
        grid=(N // tm,),
        in_specs=[row, gate_spec, row, pl.BlockSpec((tm, TOP_K), lambda i: (i, 0)),
                  pl.BlockSpec((TOP_K, tm, D), lambda i: (0, i + blk0, 0))],
        out_specs=row,
        out_shape=jax.ShapeDtypeStruct((N, D), F32),
        compiler_params=_cparams(("arbitrary",)),
        name="moe_combine",
    )(x, gate, shared, wts, gathered)


def moe_dispatch(h, idx, tm):
    N, D = h.shape
    NK = N * TOP_K
    n_blocks = -(-(NK + N_EXPERTS * (tm - 1)) // tm)
    member = jnp.sum(idx[:, :, None] == jnp.arange(N_EXPERTS)[None, None, :], axis=1).astype(jnp.int32)
    incl = jnp.cumsum(member, axis=0)
    counts = incl[-1]
    rank = jnp.take_along_axis(incl - member, idx, axis=1)
    padded = (counts + tm - 1) // tm * tm
    pend = jnp.cumsum(padded)
    pstart = pend - padded
    dest = pstart[idx] + rank
    tok = jnp.broadcast_to(jnp.arange(N, dtype=jnp.int32)[:, None], (N, TOP_K))
    rows = (jnp.arange(n_blocks * tm, dtype=jnp.int32) % N).at[dest.reshape(-1)].set(tok.reshape(-1))
    block_start = jnp.arange(n_blocks, dtype=jnp.int32) * tm
    block_e = jnp.minimum(jnp.sum(pend[None, :] <= block_start[:, None], axis=1), N_EXPERTS - 1).astype(jnp.int32)
    n_used = (pend[-1] // tm).astype(jnp.int32).reshape(1)
    return rows, block_e, n_used, dest


def moe_ffn_joint(requests, layer, p):
    h_all = jnp.concatenate([r["h"] for r in requests], axis=0)
    idx_all = jnp.concatenate([r["idx"] for r in requests], axis=0)
    n_all, D = h_all.shape
    tm = ROUTED_ROWS
    rows, block_e, n_used, dest = moe_dispatch(h_all, idx_all, tm)
    n_blocks = block_e.shape[0]
    n_first = n_blocks // 2
    src = jnp.pad(h_all, ((0, n_all), (0, 0)))
    weights = (p["exp_w_gate"], p["exp_w_up"], p["exp_w_down"])
    yb = experts(src[rows[:n_first * tm]], block_e, jnp.minimum(n_used, n_first), *weights, layer, tm, BF16,
                 "experts_routed")
    yb = experts(src[rows[n_first * tm:]], block_e[n_first:], jnp.clip(n_used - n_first, 0, n_blocks - n_first),
                 *weights, layer, tm, BF16, "experts_routed", out_rows=n_blocks * tm, out_block0=n_first, prev=yb)
    gathered = yb[dest.T.reshape(-1)].reshape(TOP_K, n_all, D)
    out, row0 = [], 0
    for r in requests:
        n = r["h"].shape[0]
        nsb = n // r["tm_shared"]
        shared = experts(r["h"], jnp.zeros((nsb,), jnp.int32), jnp.full((1,), nsb, jnp.int32),
                         p["sh_w_gate"][:, None], p["sh_w_up"][:, None], p["sh_w_down"][:, None], layer,
                         r["tm_shared"], F32, "experts_shared")
        out.append(combine(r["x"], r["gate"], shared, r["wts"], gathered, r["T"], COMBINE_ROWS, row0=row0))
        row0 += n
    return out


def trunk(x, mods, kv_mod, s0, past, p, cfg):
    B, T, D = x.shape
    N = B * T
    x = x.reshape(N, D)
    tm = cfg["tm"]
    HD = p["hg_w_out"].shape[1]
    lower = jnp.cumsum(jax.nn.softmax(p["hg_lb"].astype(F32), axis=0), axis=0)

    def split6(m):
        return [m[:, i * D:(i + 1) * D] for i in range(6)]

    sh_m, sc_m, g_m, sh_f, sc_f, g_f = split6(mods[0])
    h = norm_mod(x, p["norm_mix"][0], sh_m, sc_m, T, tm)
    mt = dict(tm=cfg["tm_mm"], tn=cfg["tn_mm"])
    qfig = mm(h, p["hg_w_in"], w_layer=0, name="hg_in", **mt)
    Tp = cfg["hg_t_pad"]
    if Tp != T:
        qfig = jnp.pad(qfig.reshape(B, T, -1), ((0, 0), (0, Tp - T), (0, 0))).reshape(B * Tp, -1)
    o, s_new = hgrn2(qfig, lower[0], p["hg_norm"][0], s0, B=B, T=Tp, C=cfg["hg_chunk"], tb=cfg["hg_tb"],
                     t_valid=None if Tp == T else T)
    if Tp != T:
        o = o.reshape(B, Tp, HD)[:, :T].reshape(N, HD)
    x = mm(o, p["hg_w_out"], w_layer=0, epilogue="resid", resid=x, gate=g_m, rows_per_group=T,
           name="hg_out", **mt)
    h, idx, wts = norm_mod(x, p["norm_ffn"][0], sh_f, sc_f, T, tm, router_w=p["router_w"],
                           router_bias=p["router_bias"], router_layer=0)
    x = yield dict(x=x, gate=g_f, T=T, h=h, idx=idx, wts=wts, tm_shared=cfg["tm_shared"])

    sh_kv, sc_kv = kv_mod[:, :D], kv_mod[:, D:]
    hk = norm_mod(x, p["kv_norm"], sh_kv, sc_kv, T, tm)
    k_new = mm(hk, p["w_kv"], n_out=HD, epilogue="headnorm", head_gain=p["k_norm"], name="k_proj", **mt)
    v_new = mm(hk, p["w_kv"], w_col0=HD, n_out=HD, name="v_proj", **mt)

    sh_m, sc_m, g_m, sh_f, sc_f, g_f = split6(mods[1])
    h = norm_mod(x, p["norm_mix"][1], sh_m, sc_m, T, tm)
    q = mm(h, p["sb_w_q"], w_layer=0, epilogue="headnorm", head_gain=p["q_norm"][0], name="q_proj", **mt)
    bias = p["sb_logit_bias"][0]
    if past is None:
        o = sb_attention_prompt(q, k_new, v_new, bias, B=B, T=T)
    else:
        o = sb_attention_sample(q, k_new, v_new, past[0], past[1], past[2], bias, B=B, T=T)
    x = mm(o, p["sb_w_out"], w_layer=0, epilogue="resid", resid=x, gate=g_m, rows_per_group=T,
           name="sb_out", **mt)
    h, idx, wts = norm_mod(x, p["norm_ffn"][1], sh_f, sc_f, T, tm, router_w=p["router_w"],
                           router_bias=p["router_bias"], router_layer=1)
    x = yield dict(x=x, gate=g_f, T=T, h=h, idx=idx, wts=wts, tm_shared=cfg["tm_shared"])
    H = HD // HEAD_DIM
    return (x.reshape(B, T, D), s_new[None], k_new.reshape(B, T, H, HEAD_DIM), v_new.reshape(B, T, H, HEAD_DIM))


PROMPT_CFG = dict(tm=512, tm_mm=1024, tn_mm=1024, hg_chunk=HG_CHUNK, hg_tb=512, hg_t_pad=None,
                  tm_shared=512)
SAMPLE_CFG = dict(tm=32, tm_mm=32, tn_mm=1024, hg_chunk=HG_CHUNK, hg_tb=HG_CHUNK, hg_t_pad=HG_CHUNK,
                  tm_shared=32)
MOD_ROWS = 16


def kernel(x_prompt, x_sample, c_prompt, c_sample, state_hgrn, cache_k, cache_v, page_table, ada_w, ada_b,
           norm_mix, norm_ffn, hg_w_in, hg_lb, hg_norm, hg_w_out, kv_ada_w, kv_ada_b, kv_norm, w_kv, k_norm,
           sb_w_q, q_norm, sb_logit_bias, sb_w_out, router_w, router_bias, exp_w_gate, exp_w_up, exp_w_down,
           sh_w_gate, sh_w_up, sh_w_down):
    p = dict(ada_w=ada_w, ada_b=ada_b, norm_mix=norm_mix, norm_ffn=norm_ffn, hg_w_in=hg_w_in, hg_lb=hg_lb,
             hg_norm=hg_norm, hg_w_out=hg_w_out, kv_ada_w=kv_ada_w, kv_ada_b=kv_ada_b, kv_norm=kv_norm,
             w_kv=w_kv, k_norm=k_norm, sb_w_q=sb_w_q, q_norm=q_norm, sb_logit_bias=sb_logit_bias,
             sb_w_out=sb_w_out, router_w=router_w, router_bias=router_bias, exp_w_gate=exp_w_gate,
             exp_w_up=exp_w_up, exp_w_down=exp_w_down, sh_w_gate=sh_w_gate, sh_w_up=sh_w_up,
             sh_w_down=sh_w_down)
    Bp, Tp, _ = x_prompt.shape
    Bs = x_sample.shape[0]
    H = hg_w_out.shape[1] // HEAD_DIM
    c_all = jnp.concatenate([c_prompt, c_sample], axis=0)
    c_all = jnp.pad(c_all, ((0, MOD_ROWS - c_all.shape[0]), (0, 0)))
    mods = [mm(c_all, ada_w, w_layer=l, bias=ada_b[l], silu_in=True, tn=1024, name="ada") for l in range(2)]
    kv_mod = mm(c_all, kv_ada_w, bias=kv_ada_b, silu_in=True, tn=1024, name="ada_kv")
    rows_p = slice(0, Bp)
    rows_s = slice(Bp, Bp + Bs)
    cfg_p = dict(PROMPT_CFG, hg_t_pad=Tp)
    s0_prompt = jnp.zeros((Bp, H, HEAD_DIM, HEAD_DIM), state_hgrn.dtype)
    groups = [trunk(x_prompt, [m[rows_p] for m in mods], kv_mod[rows_p], s0_prompt, None, p, cfg_p),
              trunk(x_sample, [m[rows_s] for m in mods], kv_mod[rows_s], state_hgrn[0],
                    (cache_k, cache_v, page_table), p, SAMPLE_CFG)]
    requests = [next(g) for g in groups]
    results = [None, None]
    layer = 0
    while any(r is None for r in results):
        new_x = moe_ffn_joint(requests, layer, p)
        for g, gen in enumerate(groups):
            try:
                requests[g] = gen.send(new_x[g])
            except StopIteration as done:
                results[g] = done.value
        layer += 1
    (y_p, s_p, k_p, v_p), (y_s, s_s, k_s, v_s) = results
    return (y_p, y_s, s_p, s_s, k_p, v_p, k_s, v_s)
```

```python
import functools
import math

import numpy as np
import jax
import jax.numpy as jnp
from jax import lax
from jax.experimental import pallas as pl
from jax.experimental.pallas import tpu as pltpu

F32 = jnp.float32
BF16 = jnp.bfloat16

LANES = 128
SAMPLE_PAGES_PER_STEP = 4
VMEM_LIMIT_BYTES = 52 * 1024 * 1024
NORM_EPS = 1e-6
HEAD_DIM = 128
N_EXPERTS = 64
TOP_K = 8
ROUTED_SCALE = 2.5
HG_CHUNK = 64
COMBINE_ROWS = 256
ROUTED_ROWS = 512


def _cparams(sem):
    return pltpu.CompilerParams(dimension_semantics=sem, vmem_limit_bytes=VMEM_LIMIT_BYTES)


def _sigmoid(x):
    return 1.0 / (1.0 + jnp.exp(-x))


def _dot(a, b):
    return jnp.dot(a, b, preferred_element_type=F32)


def _dot_nt(a, b):
    return lax.dot_general(a, b, (((1,), (1,)), ((), ())), preferred_element_type=F32)


def _dot_tn(a, b):
    return lax.dot_general(a, b, (((0,), (0,)), ((), ())), preferred_element_type=F32)


def _norm_mod_kernel(x_ref, g_ref, sh_ref, sc_ref, *rest, with_router):
    x = x_ref[...]
    ms = jnp.mean(x * x, axis=-1, keepdims=True)
    y = x * lax.rsqrt(ms + NORM_EPS) * g_ref[...]
    h = y * (1.0 + sc_ref[...]) + sh_ref[...]
    if with_router:
        rw_ref, rb_ref, o_ref, idx_ref, wt_ref = rest
        logits = jnp.dot(h, rw_ref[...], preferred_element_type=F32, precision=lax.Precision.HIGHEST)
        scores = _sigmoid(logits)
        sel = scores + rb_ref[...]
        tm, E = sel.shape
        lane = lax.broadcasted_iota(jnp.int32, (tm, E), 1)
        slot = lax.broadcasted_iota(jnp.int32, (tm, TOP_K), 1)
        idx = jnp.zeros((tm, TOP_K), jnp.int32)
        wts = jnp.zeros((tm, TOP_K), F32)
        for k in range(TOP_K):
            best = jnp.max(sel, axis=-1, keepdims=True)
            pick = jnp.min(jnp.where(sel == best, lane, E), axis=-1, keepdims=True)
            hit = lane == pick
            w_k = jnp.sum(jnp.where(hit, scores, 0.0), axis=-1, keepdims=True)
            sel = jnp.where(hit, -jnp.inf, sel)
            idx = jnp.where(slot == k, pick, idx)
            wts = jnp.where(slot == k, w_k, wts)
        idx_ref[...] = idx
        wt_ref[...] = wts / jnp.sum(wts, axis=-1, keepdims=True) * ROUTED_SCALE
    else:
        o_ref, = rest
    o_ref[...] = h.astype(o_ref.dtype)


def norm_mod(x, gain, shift, scale, rows_per_group, tm, router_w=None, router_bias=None, router_layer=None):
    N, D = x.shape
    tm = min(tm, N)
    with_router = router_w is not None
    if rows_per_group % tm == 0:
        bpg = rows_per_group // tm
        sh = shift.reshape(-1, 1, D)
        sc = scale.reshape(-1, 1, D)
        mod_spec = pl.BlockSpec((None, 1, D), lambda i: (i // bpg, 0, 0))
    else:
        sh = jnp.repeat(shift, rows_per_group, axis=0)
        sc = jnp.repeat(scale, rows_per_group, axis=0)
        mod_spec = pl.BlockSpec((tm, D), lambda i: (i, 0))
    args = [x, gain.reshape(1, D), sh, sc]
    in_specs = [pl.BlockSpec((tm, D), lambda i: (i, 0)),
                pl.BlockSpec((1, D), lambda i: (0, 0)),
                mod_spec, mod_spec]
    out_specs = pl.BlockSpec((tm, D), lambda i: (i, 0))
    out_shape = jax.ShapeDtypeStruct((N, D), BF16)
    if with_router:
        E = router_w.shape[-1]
        args += [router_w, router_bias.astype(F32).reshape(-1, 1, E)]
        in_specs += [pl.BlockSpec((None, D, E), lambda i: (router_layer, 0, 0)),
                     pl.BlockSpec((None, 1, E), lambda i: (router_layer, 0, 0))]
        out_specs = [out_specs, pl.BlockSpec((tm, TOP_K), lambda i: (i, 0)),
                     pl.BlockSpec((tm, TOP_K), lambda i: (i, 0))]
        out_shape = [out_shape, jax.ShapeDtypeStruct((N, TOP_K), jnp.int32),
                     jax.ShapeDtypeStruct((N, TOP_K), F32)]
    return pl.pallas_call(
        functools.partial(_norm_mod_kernel, with_router=with_router),
        grid=(N // tm,),
        in_specs=in_specs,
        out_specs=out_specs,
        out_shape=out_shape,
        compiler_params=_cparams(("arbitrary",)),
        name="norm_router" if with_router else "norm_mod",
    )(*args)


def _mm_kernel(*refs, silu_in, has_bias, epilogue, tn):
    it = iter(refs)
    x_ref = next(it)
    w_ref = next(it)
    b_ref = next(it) if has_bias else None
    if epilogue == "headnorm":
        hg_ref = next(it)
    elif epilogue == "resid":
        r_ref = next(it)
        gate_ref = next(it)
    o_ref = next(it)
    wbf_ref = next(it)

    @pl.when(pl.program_id(1) == 0)
    def _():
        wbf_ref[...] = w_ref[...].astype(BF16)

    x = x_ref[...]
    if silu_in:
        x = x.astype(F32)
        x = x * _sigmoid(x)
    y = _dot(x.astype(BF16), wbf_ref[...])
    if has_bias:
        y = y + b_ref[...]
    if epilogue == "headnorm":
        gain = hg_ref[...]
        for c in range(tn // HEAD_DIM):
            sl = slice(c * HEAD_DIM, (c + 1) * HEAD_DIM)
            yc = y[:, sl]
            ms = jnp.mean(yc * yc, axis=-1, keepdims=True)
            o_ref[:, sl] = (yc * lax.rsqrt(ms + NORM_EPS) * gain).astype(o_ref.dtype)
    elif epilogue == "resid":
        o_ref[...] = (r_ref[...] + gate_ref[...] * y).astype(o_ref.dtype)
    else:
        o_ref[...] = y.astype(o_ref.dtype)


def mm(x, w, *, w_layer=None, w_col0=0, n_out=None, bias=None, silu_in=False, out_dtype=F32,
       tm=512, tn=512, epilogue=None, head_gain=None, resid=None, gate=None, rows_per_group=None,
       name="mm"):
    M, K = x.shape
    Ntot = w.shape[-1]
    N = Ntot - w_col0 if n_out is None else n_out
    tm = min(tm, M)
    tn = min(tn, N)
    assert M % tm == 0 and N % tn == 0 and w_col0 % tn == 0
    c0 = w_col0 // tn
    if w.ndim == 3:
        w_spec = pl.BlockSpec((None, K, tn), lambda j, i: (w_layer, 0, j + c0))
    else:
        w_spec = pl.BlockSpec((K, tn), lambda j, i: (0, j + c0))
    args = [x, w]
    in_specs = [pl.BlockSpec((tm, K), lambda j, i: (i, 0)), w_spec]
    if bias is not None:
        args.append(bias.reshape(1, -1))
        in_specs.append(pl.BlockSpec((1, tn), lambda j, i: (0, j + c0)))
    if epilogue == "headnorm":
        args.append(head_gain.reshape(1, HEAD_DIM))
        in_specs.append(pl.BlockSpec((1, HEAD_DIM), lambda j, i: (0, 0)))
    elif epilogue == "resid":
        args.append(resid)
        in_specs.append(pl.BlockSpec((tm, tn), lambda j, i: (i, j)))
        if rows_per_group % tm == 0:
            bpg = rows_per_group // tm
            args.append(gate.reshape(-1, 1, N))
            in_specs.append(pl.BlockSpec((None, 1, tn), lambda j, i: (i // bpg, 0, j)))
        else:
            args.append(jnp.repeat(gate, rows_per_group, axis=0))
            in_specs.append(pl.BlockSpec((tm, tn), lambda j, i: (i, j)))
    kern = functools.partial(_mm_kernel, silu_in=silu_in, has_bias=bias is not None,
                             epilogue=epilogue, tn=tn)
    return pl.pallas_call(
        kern,
        grid=(N // tn, M // tm),
        in_specs=in_specs,
        out_specs=pl.BlockSpec((tm, tn), lambda j, i: (i, j)),
        out_shape=jax.ShapeDtypeStruct((M, N), out_dtype),
        scratch_shapes=[pltpu.VMEM((K, tn), BF16)],
        compiler_params=_cparams(("arbitrary", "arbitrary")),
        cost_estimate=pl.CostEstimate(flops=2 * M * K * N, transcendentals=0,
                                      bytes_accessed=(N // tn) * M * K * x.dtype.itemsize + 4 * K * N + 4 * M * N),
        name=name,
    )(*args)


def _hgrn_constants(C):
    r = np.arange(C)
    mats = [(r[None, :] <= r[:, None]).astype(np.float32)]
    masks = [np.eye(C, dtype=np.float32)]
    m = C // 2
    while m >= 1:
        base = (r // (2 * m)) * (2 * m)
        pivot = base + m - 1
        upper = r > pivot
        j = r[None, :]
        mat = np.where(upper[:, None], (j > pivot[:, None]) & (j <= r[:, None]),
                       (j > r[:, None]) & (j <= pivot[:, None]))
        mats.append(mat.astype(np.float32))
        same = base[:, None] == base[None, :]
        masks.append((same & upper[:, None] & (~upper)[None, :]).astype(np.float32))
        m //= 2
    return np.concatenate(mats, axis=0), np.stack(masks, axis=0)


def _hgrn_kernel(q_ref, fu_ref, i_ref, g_ref, lb_ref, gain_ref, s0_ref, mat_ref, mask_ref,
                 o_ref, sout_ref, s_ref, *, C, n_chunks, n_levels, t_valid):
    t = pl.program_id(2)

    @pl.when(t == 0)
    def _():
        s_ref[...] = s0_ref[...].T

    lb = lb_ref[...]
    gain = gain_ref[...]
    staged = []
    for c in range(n_chunks):
        rows = slice(c * C, (c + 1) * C)
        q = q_ref[rows, :]
        f = lb + (1.0 - lb) * _sigmoid(fu_ref[rows, :])
        logf = jnp.log(f)
        kk = 1.0 - f
        if t_valid is not None:
            valid = (lax.broadcasted_iota(jnp.int32, (C, HEAD_DIM), 0) + c * C) < t_valid
            logf = jnp.where(valid, logf, 0.0)
            kk = jnp.where(valid, kk, 0.0)
        hi = logf.astype(BF16)
        lo = (logf - hi.astype(F32)).astype(BF16)
        e = _dot(mat_ref[...], hi) + _dot(mat_ref[...], lo)
        b = e[0:C, :]
        scores = mask_ref[0] * _dot_nt(q.astype(BF16), kk.astype(BF16))
        for l in range(1, n_levels + 1):
            x = jnp.exp(e[l * C:(l + 1) * C, :])
            scores = scores + mask_ref[l] * _dot_nt((q * x).astype(BF16), (kk * x).astype(BF16))
        b_end = b[C - 1:C, :]
        staged.append(((q * jnp.exp(b)).astype(BF16), scores.astype(BF16), i_ref[rows, :].astype(BF16),
                       (kk * jnp.exp(b_end - b)).astype(BF16), jnp.exp(b_end)))

    St = s_ref[...]
    for c, (q_dec, scores, vb, khat, decay) in enumerate(staged):
        rows = slice(c * C, (c + 1) * C)
        o = _dot_nt(q_dec, St.astype(BF16)) + _dot(scores, vb)
        St = decay * St + _dot_tn(vb, khat)
        ms = jnp.mean(o * o, axis=-1, keepdims=True)
        gt = g_ref[rows, :]
        o_ref[rows, :] = (o * lax.rsqrt(ms + NORM_EPS) * gain * (gt * _sigmoid(gt))).astype(o_ref.dtype)
    s_ref[...] = St

    @pl.when(t == pl.num_programs(2) - 1)
    def _():
        sout_ref[...] = s_ref[...].T


def hgrn2(qfig, lb, gain, s0, *, B, T, C, tb, t_valid=None):
    H = s0.shape[1]
    HD = H * HEAD_DIM
    x3 = qfig.reshape(B, T, 4 * HD)
    mats, masks = _hgrn_constants(C)
    n_levels = masks.shape[0] - 1
    kern = functools.partial(_hgrn_kernel, C=C, n_chunks=tb // C, n_levels=n_levels, t_valid=t_valid)

    def col(off):
        return pl.BlockSpec((None, tb, HEAD_DIM), lambda b, h, t: (b, t, h + off * H))

    head_vec = pl.BlockSpec((1, HEAD_DIM), lambda b, h, t: (0, h))
    o, s_out = pl.pallas_call(
        kern,
        grid=(B, H, T // tb),
        in_specs=[col(0), col(1), col(2), col(3), head_vec, head_vec,
                  pl.BlockSpec((None, None, HEAD_DIM, HEAD_DIM), lambda b, h, t: (b, h, 0, 0)),
                  pl.BlockSpec(mats.shape, lambda b, h, t: (0, 0)),
                  pl.BlockSpec(masks.shape, lambda b, h, t: (0, 0, 0))],
        out_specs=[pl.BlockSpec((None, tb, HEAD_DIM), lambda b, h, t: (b, t, h)),
                   pl.BlockSpec((None, None, HEAD_DIM, HEAD_DIM), lambda b, h, t: (b, h, 0, 0))],
        out_shape=[jax.ShapeDtypeStruct((B, T, HD), BF16),
                   jax.ShapeDtypeStruct(s0.shape, F32)],
        scratch_shapes=[pltpu.VMEM((HEAD_DIM, HEAD_DIM), F32)],
        compiler_params=_cparams(("arbitrary", "arbitrary", "arbitrary")),
        cost_estimate=pl.CostEstimate(flops=2 * B * T * H * HEAD_DIM * (3 * C * (n_levels + 1) + 4 * HEAD_DIM),
                                      transcendentals=B * T * HD * (n_levels + 8),
                                      bytes_accessed=B * T * HD * 18),
        name="hgrn2",
    )(x3, x3, x3, x3, lb.reshape(1, HD), gain.reshape(1, HD), s0,
      jnp.asarray(mats, BF16), jnp.asarray(masks))
    return o.reshape(B * T, HD), s_out


LOG2E = math.log2(math.e)
SIGN_BIT = 0x80000000


def _suffix_sum(x, u):
    hi = x.astype(BF16)
    lo = (x - hi.astype(F32)).astype(BF16)
    return _dot(hi, u) + _dot(lo, u)


def _sb_logs(w, visible):
    neg_abs = lax.bitcast_convert_type(lax.bitcast_convert_type(w, jnp.uint32) | jnp.uint32(SIGN_BIT), F32)
    drop = jnp.maximum(w, 0.0) + jnp.log(1.0 + jnp.exp2(neg_abs)) * LOG2E
    log_sig = w - drop
    if visible is not None:
        drop = jnp.where(visible, drop, 0.0)
    return log_sig, drop


def _sb_weights(log_sig, drop, visible, u, newer):
    a = jnp.exp2(log_sig - (_suffix_sum(drop, u) + newer))
    if visible is not None:
        a = jnp.where(visible, a, 0.0)
    return a


def _sb_block(w, visible, u, rsum):
    log_sig, drop = _sb_logs(w, visible)
    return _sb_weights(log_sig, drop, visible, u, rsum), rsum + jnp.sum(drop, axis=-1, keepdims=True)


def _sb_prompt_kernel(bias_ref, q_ref, k_ref, v_ref, u_ref, o_ref, *, tq, tk, scale):
    h = pl.program_id(1)
    qi = pl.program_id(2)
    bias2 = bias_ref[h] * LOG2E
    q = q_ref[...].astype(BF16)
    u = u_ref[...]
    nd = tq // tk

    def block(kb, visible, rsum, acc):
        start = pl.multiple_of(kb * tk, tk)
        k = k_ref[pl.ds(start, tk), :].astype(BF16)
        v = v_ref[pl.ds(start, tk), :].astype(BF16)
        w = _dot_nt(q, k) * (scale * LOG2E) + bias2
        a, rsum = _sb_block(w, visible, u, rsum)
        return rsum, acc + _dot(a.astype(BF16), v)

    row = lax.broadcasted_iota(jnp.int32, (tq, tk), 0)
    col = lax.broadcasted_iota(jnp.int32, (tq, tk), 1)
    carry = (jnp.zeros((tq, 1), F32), jnp.zeros((tq, HEAD_DIM), F32))
    for d in reversed(range(nd)):
        carry = block(qi * nd + d, col + d * tk < row, *carry)

    def body(it, carry):
        rsum, acc = carry
        ws, vs = [], []
        for d in range(nd):
            start = pl.multiple_of(((qi - it) * nd - 1 - d) * tk, tk)
            ws.append(_dot_nt(q, k_ref[pl.ds(start, tk), :].astype(BF16)) * (scale * LOG2E) + bias2)
            vs.append(v_ref[pl.ds(start, tk), :].astype(BF16))
        log_sig, drop = _sb_logs(jnp.concatenate(ws, axis=0), None)
        block_sum = jnp.sum(drop, axis=-1, keepdims=True)
        newer = []
        for d in range(nd):
            newer.append(rsum)
            rsum = rsum + block_sum[d * tq:(d + 1) * tq]
        a = _sb_weights(log_sig, drop, None, u, jnp.concatenate(newer, axis=0)).astype(BF16)
        for d in range(nd):
            acc = acc + _dot(a[d * tq:(d + 1) * tq], vs[d])
        return rsum, acc

    rsum, acc = lax.fori_loop(0, qi, body, carry)
    o_ref[...] = acc.astype(o_ref.dtype)


def sb_attention_prompt(q, k, v, bias, *, B, T, tq=512, tk=256):
    N, HD = q.shape
    H = HD // HEAD_DIM
    tq = min(tq, T)
    tk = min(tk, tq)
    nq = T // tq
    u = jnp.asarray(np.triu(np.ones((tk, tk), np.float32), 0).T - np.eye(tk, dtype=np.float32), BF16)
    kern = functools.partial(_sb_prompt_kernel, tq=tq, tk=tk, scale=HEAD_DIM ** -0.5)
    return pl.pallas_call(
        kern,
        grid_spec=pltpu.PrefetchScalarGridSpec(
            num_scalar_prefetch=1,
            grid=(B, H, nq),
            in_specs=[pl.BlockSpec((tq, HEAD_DIM), lambda b, h, i, bias: (b * nq + i, h)),
                      pl.BlockSpec((T, HEAD_DIM), lambda b, h, i, bias: (b, h)),
                      pl.BlockSpec((T, HEAD_DIM), lambda b, h, i, bias: (b, h)),
                      pl.BlockSpec((tk, tk), lambda b, h, i, bias: (0, 0))],
            out_specs=pl.BlockSpec((tq, HEAD_DIM), lambda b, h, i, bias: (b * nq + i, h)),
        ),
        out_shape=jax.ShapeDtypeStruct((N, HD), BF16),
        compiler_params=_cparams(("arbitrary", "arbitrary", "arbitrary")),
        cost_estimate=pl.CostEstimate(flops=B * H * T * T * (2 * HEAD_DIM + 2 * tk), transcendentals=3 * B * H * T * T // 2,
                                      bytes_accessed=N * HD * (2 * 4 + 4 + 2)),
        name="sb_attn_prompt",
    )(bias.astype(F32), q, k, v, u)


def _sb_sample_kernel(pt_ref, q_ref, bias_ref, u_ref, new_k, new_v, *rest, c1, G, tq):
    pages = rest[:2 * G]
    o_ref, rsum_ref, acc_ref = rest[2 * G:]
    j = pl.program_id(1)
    R = q_ref.shape[0]
    H = R // tq
    n_tiles = new_k.shape[0] // LANES
    keys_per_tile = LANES // H

    @pl.when(j == 0)
    def _():
        rsum_ref[...] = jnp.zeros_like(rsum_ref)
        acc_ref[...] = jnp.zeros_like(acc_ref)

    row = lax.broadcasted_iota(jnp.int32, (R, LANES), 0)
    lane = lax.broadcasted_iota(jnp.int32, (R, LANES), 1)
    own_head = (lane % H) == (row // tq)

    def stack(tiles):
        return jnp.concatenate(tiles, axis=0)

    def page(k_ref, v_ref, causal, rsum):
        w = _dot_nt(q_ref[...], k_ref[...].astype(BF16)) * c1 + bias_ref[...]
        ws = stack([w[:, t * LANES:(t + 1) * LANES] for t in range(n_tiles)])
        if causal:
            visible = stack([jnp.logical_and(own_head, t * keys_per_tile + lane // H < row % tq)
                             for t in range(n_tiles)])
        else:
            visible = stack([own_head] * n_tiles)
        log_sig, drop = _sb_logs(ws, visible)
        tile_sum = jnp.sum(drop, axis=-1, keepdims=True)
        newer = [None] * n_tiles
        for t in reversed(range(n_tiles)):
            newer[t] = rsum
            rsum = rsum + tile_sum[t * R:(t + 1) * R]
        a = _sb_weights(log_sig, drop, visible, u_ref[...], stack(newer)).astype(BF16)
        a = jnp.concatenate([a[t * R:(t + 1) * R] for t in range(n_tiles)], axis=1)
        return rsum, _dot(a, v_ref[...].astype(BF16))

    def run(blocks):
        rsum = rsum_ref[...]
        acc = acc_ref[...]
        for k_ref, v_ref, causal in blocks:
            rsum, o = page(k_ref, v_ref, causal, rsum)
            acc = acc + o
        rsum_ref[...] = rsum
        acc_ref[...] = acc

    @pl.when(j == 0)
    def _():
        run([(new_k, new_v, True)])

    @pl.when(j > 0)
    def _():
        run([(pages[2 * g], pages[2 * g + 1], False) for g in range(G)])

    @pl.when(j == pl.num_programs(1) - 1)
    def _():
        o_ref[...] = acc_ref[...]


def sb_attention_sample(q, k_new, v_new, cache_k, cache_v, page_table, bias, *, B, T):
    HD = q.shape[1]
    H = HD // HEAD_DIM
    n_pages = page_table.shape[1]
    P = cache_k.shape[1]
    R = H * T
    assert LANES % H == 0 and n_pages % SAMPLE_PAGES_PER_STEP == 0
    qh = q.reshape(B, T, H, HEAD_DIM).transpose(0, 2, 1, 3).reshape(B, R, HEAD_DIM)
    bias_rows = (jnp.repeat(bias.astype(F32), T) * LOG2E).reshape(R, 1)
    u = jnp.asarray(np.triu(np.ones((LANES, LANES), np.float32), 0).T - np.eye(LANES, dtype=np.float32), BF16)
    G = SAMPLE_PAGES_PER_STEP
    page_view = lambda a: a.reshape(a.shape[0], P * H, HEAD_DIM)
    new_view = lambda a: page_view(jnp.pad(a.reshape(B, T, H, HEAD_DIM), ((0, 0), (0, P - T), (0, 0), (0, 0))))
    page = (None, P * H, HEAD_DIM)

    def page_spec(g):
        return pl.BlockSpec(page, lambda b, j, pt: (pt[b, n_pages - 1 - (jnp.maximum(j, 1) - 1) * G - g], 0, 0))

    in_specs = [pl.BlockSpec((None, R, HEAD_DIM), lambda b, j, pt: (b, 0, 0)),
                pl.BlockSpec((R, 1), lambda b, j, pt: (0, 0)),
                pl.BlockSpec((LANES, LANES), lambda b, j, pt: (0, 0)),
                pl.BlockSpec(page, lambda b, j, pt: (b, 0, 0)),
                pl.BlockSpec(page, lambda b, j, pt: (b, 0, 0))]
    args = [page_table, qh.astype(BF16), bias_rows, u, new_view(k_new), new_view(v_new)]
    ck, cv = page_view(cache_k), page_view(cache_v)
    for g in range(G):
        in_specs += [page_spec(g), page_spec(g)]
        args += [ck, cv]
    o = pl.pallas_call(
        functools.partial(_sb_sample_kernel, c1=HEAD_DIM ** -0.5 * LOG2E, G=G, tq=T),
        grid_spec=pltpu.PrefetchScalarGridSpec(
            num_scalar_prefetch=1,
            grid=(B, n_pages // G + 1),
            in_specs=in_specs,
            out_specs=pl.BlockSpec((None, R, HEAD_DIM), lambda b, j, pt: (b, 0, 0)),
            scratch_shapes=[pltpu.VMEM((R, 1), F32), pltpu.VMEM((R, HEAD_DIM), F32)],
        ),
        out_shape=jax.ShapeDtypeStruct((B, R, HEAD_DIM), F32),
        compiler_params=_cparams(("arbitrary", "arbitrary")),
        cost_estimate=pl.CostEstimate(flops=B * (n_pages + 1) * P * H * R * (4 * HEAD_DIM + 4 * LANES),
                                      transcendentals=3 * B * (n_pages + 1) * P * H * R,
                                      bytes_accessed=2 * B * (n_pages + 1) * P * HD * 4),
        name="sb_attn_sample",
    )(*args)
    return o.reshape(B, H, T, HEAD_DIM).transpose(0, 2, 1, 3).reshape(B * T, HD).astype(BF16)


def _experts_kernel(be_ref, nb_ref, first_ref, slot_ref, next_ref, x_ref, wg_hbm, wu_hbm, wd_hbm, o_ref,
                    wg_buf, wu_buf, wd_buf, wgb_ref, wub_ref, wdb_ref, sem, *, layer):
    i = pl.program_id(0)

    def weight_copies(e, s):
        return (pltpu.make_async_copy(wg_hbm.at[layer, e], wg_buf.at[s], sem.at[0, s]),
                pltpu.make_async_copy(wu_hbm.at[layer, e], wu_buf.at[s], sem.at[1, s]),
                pltpu.make_async_copy(wd_hbm.at[layer, e], wd_buf.at[s], sem.at[2, s]))

    @pl.when(i == 0)
    def _():
        for c in weight_copies(be_ref[0], 0):
            c.start()

    @pl.when(first_ref[i] == 1)
    def _():
        s = slot_ref[i]
        for c in weight_copies(be_ref[i], s):
            c.wait()

        @pl.when(next_ref[i] >= 0)
        def _():
            for c in weight_copies(next_ref[i], 1 - s):
                c.start()

        wgb_ref[...] = wg_buf[s].astype(BF16)
        wub_ref[...] = wu_buf[s].astype(BF16)
        wdb_ref[...] = wd_buf[s].astype(BF16)

    @pl.when(i < nb_ref[0])
    def _():
        x = x_ref[...]
        g = _dot(x, wgb_ref[...])
        up = _dot(x, wub_ref[...])
        a = (g * _sigmoid(g) * up).astype(BF16)
        o_ref[...] = _dot(a, wdb_ref[...]).astype(o_ref.dtype)

    @pl.when(i >= nb_ref[0])
    def _():
        o_ref[...] = jnp.zeros_like(o_ref)


def experts(x, block_e, n_used, wg, wu, wd, layer, tm, out_dtype, name):
    NR, D = x.shape
    Fd = wg.shape[-1]
    n_blocks = NR // tm
    pos = jnp.arange(n_blocks, dtype=jnp.int32)
    prev_e = jnp.concatenate([block_e[:1], block_e[:-1]])
    first = jnp.logical_and(pos < n_used[0], jnp.logical_or(pos == 0, block_e != prev_e))
    slot = (jnp.cumsum(first.astype(jnp.int32)) - 1) % 2
    first_pos = jnp.where(first, pos, n_blocks)
    next_first = lax.cummin(jnp.concatenate([first_pos[1:], jnp.full((1,), n_blocks, jnp.int32)]), reverse=True)
    next_e = jnp.where(next_first < n_blocks, block_e[jnp.minimum(next_first, n_blocks - 1)], -1)
    hbm = pl.BlockSpec(memory_space=pl.ANY)
    idx = lambda i, *_: (i, 0)
    return pl.pallas_call(
        functools.partial(_experts_kernel, layer=layer),
        grid_spec=pltpu.PrefetchScalarGridSpec(
            num_scalar_prefetch=5,
            grid=(n_blocks,),
            in_specs=[pl.BlockSpec((tm, D), idx), hbm, hbm, hbm],
            out_specs=pl.BlockSpec((tm, D), idx),
            scratch_shapes=[pltpu.VMEM((2, D, Fd), F32), pltpu.VMEM((2, D, Fd), F32), pltpu.VMEM((2, Fd, D), F32),
                            pltpu.VMEM((D, Fd), BF16), pltpu.VMEM((D, Fd), BF16), pltpu.VMEM((Fd, D), BF16),
                            pltpu.SemaphoreType.DMA((3, 2))],
        ),
        out_shape=jax.ShapeDtypeStruct((NR, D), out_dtype),
        compiler_params=_cparams(("arbitrary",)),
        cost_estimate=pl.CostEstimate(flops=6 * NR * D * Fd, transcendentals=NR * Fd,
                                      bytes_accessed=min(n_blocks, wg.shape[1]) * 12 * D * Fd + 4 * NR * D),
        name=name,
    )(block_e, n_used, first.astype(jnp.int32), slot.astype(jnp.int32), next_e.astype(jnp.int32), x, wg, wu, wd)


def _combine_kernel(x_ref, gate_ref, sh_ref, wt_ref, g_ref, o_ref):
    wts = wt_ref[...]
    y = sh_ref[...]
    for k in range(TOP_K):
        y = y + wts[:, k:k + 1] * g_ref[k].astype(F32)
    o_ref[...] = x_ref[...] + gate_ref[...] * y


def combine(x, gate, shared, wts, gathered, rows_per_group, tm, row0=0):
    N, D = x.shape
    tm = min(tm, N)
    assert row0 % tm == 0
    blk0 = row0 // tm
    if rows_per_group % tm == 0:
        bpg = rows_per_group // tm
        gate = gate.reshape(-1, 1, D)
        gate_spec = pl.BlockSpec((None, 1, D), lambda i: (i // bpg, 0, 0))
    else:
        gate = jnp.repeat(gate, rows_per_group, axis=0)
        gate_spec = pl.BlockSpec((tm, D), lambda i: (i, 0))
    row = pl.BlockSpec((tm, D), lambda i: (i, 0))
    return pl.pallas_call(
        _combine_kernel,
        grid=(N // tm,),
        in_specs=[row, gate_spec, row, pl.BlockSpec((tm, TOP_K), lambda i: (i, 0)),
                  pl.BlockSpec((TOP_K, tm, D), lambda i: (0, i + blk0, 0))],
        out_specs=row,
        out_shape=jax.ShapeDtypeStruct((N, D), F32),
        compiler_params=_cparams(("arbitrary",)),
        name="moe_combine",
    )(x, gate, shared, wts, gathered)


def moe_dispatch(h, idx, tm):
    N, D = h.shape
    NK = N * TOP_K
    n_blocks = -(-(NK + N_EXPERTS * (tm - 1)) // tm)
    member = jnp.sum(idx[:, :, None] == jnp.arange(N_EXPERTS)[None, None, :], axis=1).astype(jnp.int32)
    incl = jnp.cumsum(member, axis=0)
    counts = incl[-1]
    rank = jnp.take_along_axis(incl - member, idx, axis=1)
    padded = (counts + tm - 1) // tm * tm
    pend = jnp.cumsum(padded)
    pstart = pend - padded
    dest = pstart[idx] + rank
    tok = jnp.broadcast_to(jnp.arange(N, dtype=jnp.int32)[:, None], (N, TOP_K))
    rows = (jnp.arange(n_blocks * tm, dtype=jnp.int32) % N).at[dest.reshape(-1)].set(tok.reshape(-1))
    block_start = jnp.arange(n_blocks, dtype=jnp.int32) * tm
    block_e = jnp.minimum(jnp.sum(pend[None, :] <= block_start[:, None], axis=1), N_EXPERTS - 1).astype(jnp.int32)
    n_used = (pend[-1] // tm).astype(jnp.int32).reshape(1)
    xg = jnp.pad(h, ((0, N), (0, 0)))[rows]
    return xg, block_e, n_used, dest


def moe_ffn_joint(requests, layer, p):
    h_all = jnp.concatenate([r["h"] for r in requests], axis=0)
    idx_all = jnp.concatenate([r["idx"] for r in requests], axis=0)
    n_all, D = h_all.shape
    xg, block_e, n_used, dest = moe_dispatch(h_all, idx_all, ROUTED_ROWS)
    yb = experts(xg, block_e, n_used, p["exp_w_gate"], p["exp_w_up"], p["exp_w_down"], layer, ROUTED_ROWS, BF16,
                 "experts_routed")
    gathered = yb[dest.T.reshape(-1)].reshape(TOP_K, n_all, D)
    out, row0 = [], 0
    for r in requests:
        n = r["h"].shape[0]
        nsb = n // r["tm_shared"]
        shared = experts(r["h"], jnp.zeros((nsb,), jnp.int32), jnp.full((1,), nsb, jnp.int32),
                         p["sh_w_gate"][:, None], p["sh_w_up"][:, None], p["sh_w_down"][:, None], layer,
                         r["tm_shared"], F32, "experts_shared")
        out.append(combine(r["x"], r["gate"], shared, r["wts"], gathered, r["T"], COMBINE_ROWS, row0=row0))
        row0 += n
    return out


def trunk(x, mods, kv_mod, s0, past, p, cfg):
    B, T, D = x.shape
    N = B * T
    x = x.reshape(N, D)
    tm = cfg["tm"]
    HD = p["hg_w_out"].shape[1]
    lower = jnp.cumsum(jax.nn.softmax(p["hg_lb"].astype(F32), axis=0), axis=0)

    def split6(m):
        return [m[:, i * D:(i + 1) * D] for i in range(6)]

    sh_m, sc_m, g_m, sh_f, sc_f, g_f = split6(mods[0])
    h = norm_mod(x, p["norm_mix"][0], sh_m, sc_m, T, tm)
    mt = dict(tm=cfg["tm_mm"], tn=cfg["tn_mm"])
    qfig = mm(h, p["hg_w_in"], w_layer=0, name="hg_in", **mt)
    Tp = cfg["hg_t_pad"]
    if Tp != T:
        qfig = jnp.pad(qfig.reshape(B, T, -1), ((0, 0), (0, Tp - T), (0, 0))).reshape(B * Tp, -1)
    o, s_new = hgrn2(qfig, lower[0], p["hg_norm"][0], s0, B=B, T=Tp, C=cfg["hg_chunk"], tb=cfg["hg_tb"],
                     t_valid=None if Tp == T else T)
    if Tp != T:
        o = o.reshape(B, Tp, HD)[:, :T].reshape(N, HD)
    x = mm(o, p["hg_w_out"], w_layer=0, epilogue="resid", resid=x, gate=g_m, rows_per_group=T,
           name="hg_out", **mt)
    h, idx, wts = norm_mod(x, p["norm_ffn"][0], sh_f, sc_f, T, tm, router_w=p["router_w"],
                           router_bias=p["router_bias"], router_layer=0)
    x = yield dict(x=x, gate=g_f, T=T, h=h, idx=idx, wts=wts, tm_shared=cfg["tm_shared"])

    sh_kv, sc_kv = kv_mod[:, :D], kv_mod[:, D:]
    hk = norm_mod(x, p["kv_norm"], sh_kv, sc_kv, T, tm)
    k_new = mm(hk, p["w_kv"], n_out=HD, epilogue="headnorm", head_gain=p["k_norm"], name="k_proj", **mt)
    v_new = mm(hk, p["w_kv"], w_col0=HD, n_out=HD, name="v_proj", **mt)

    sh_m, sc_m, g_m, sh_f, sc_f, g_f = split6(mods[1])
    h = norm_mod(x, p["norm_mix"][1], sh_m, sc_m, T, tm)
    q = mm(h, p["sb_w_q"], w_layer=0, epilogue="headnorm", head_gain=p["q_norm"][0], name="q_proj", **mt)
    bias = p["sb_logit_bias"][0]
    if past is None:
        o = sb_attention_prompt(q, k_new, v_new, bias, B=B, T=T)
    else:
        o = sb_attention_sample(q, k_new, v_new, past[0], past[1], past[2], bias, B=B, T=T)
    x = mm(o, p["sb_w_out"], w_layer=0, epilogue="resid", resid=x, gate=g_m, rows_per_group=T,
           name="sb_out", **mt)
    h, idx, wts = norm_mod(x, p["norm_ffn"][1], sh_f, sc_f, T, tm, router_w=p["router_w"],
                           router_bias=p["router_bias"], router_layer=1)
    x = yield dict(x=x, gate=g_f, T=T, h=h, idx=idx, wts=wts, tm_shared=cfg["tm_shared"])
    H = HD // HEAD_DIM
    return (x.reshape(B, T, D), s_new[None], k_new.reshape(B, T, H, HEAD_DIM), v_new.reshape(B, T, H, HEAD_DIM))


PROMPT_CFG = dict(tm=512, tm_mm=1024, tn_mm=1024, hg_chunk=HG_CHUNK, hg_tb=512, hg_t_pad=None,
                  tm_shared=512)
SAMPLE_CFG = dict(tm=32, tm_mm=32, tn_mm=1024, hg_chunk=HG_CHUNK, hg_tb=HG_CHUNK, hg_t_pad=HG_CHUNK,
                  tm_shared=32)
MOD_ROWS = 16


def kernel(x_prompt, x_sample, c_prompt, c_sample, state_hgrn, cache_k, cache_v, page_table, ada_w, ada_b,
           norm_mix, norm_ffn, hg_w_in, hg_lb, hg_norm, hg_w_out, kv_ada_w, kv_ada_b, kv_norm, w_kv, k_norm,
           sb_w_q, q_norm, sb_logit_bias, sb_w_out, router_w, router_bias, exp_w_gate, exp_w_up, exp_w_down,
           sh_w_gate, sh_w_up, sh_w_down):
    p = dict(ada_w=ada_w, ada_b=ada_b, norm_mix=norm_mix, norm_ffn=norm_ffn, hg_w_in=hg_w_in, hg_lb=hg_lb,
             hg_norm=hg_norm, hg_w_out=hg_w_out, kv_ada_w=kv_ada_w, kv_ada_b=kv_ada_b, kv_norm=kv_norm,
             w_kv=w_kv, k_norm=k_norm, sb_w_q=sb_w_q, q_norm=q_norm, sb_logit_bias=sb_logit_bias,
             sb_w_out=sb_w_out, router_w=router_w, router_bias=router_bias, exp_w_gate=exp_w_gate,
             exp_w_up=exp_w_up, exp_w_down=exp_w_down, sh_w_gate=sh_w_gate, sh_w_up=sh_w_up,
             sh_w_down=sh_w_down)
    Bp, Tp, _ = x_prompt.shape
    Bs = x_sample.shape[0]
    H = hg_w_out.shape[1] // HEAD_DIM
    c_all = jnp.concatenate([c_prompt, c_sample], axis=0)
    c_all = jnp.pad(c_all, ((0, MOD_ROWS - c_all.shape[0]), (0, 0)))
    mods = [mm(c_all, ada_w, w_layer=l, bias=ada_b[l], silu_in=True, tn=1024, name="ada") for l in range(2)]
    kv_mod = mm(c_all, kv_ada_w, bias=kv_ada_b, silu_in=True, tn=1024, name="ada_kv")
    rows_p = slice(0, Bp)
    rows_s = slice(Bp, Bp + Bs)
    cfg_p = dict(PROMPT_CFG, hg_t_pad=Tp)
    s0_prompt = jnp.zeros((Bp, H, HEAD_DIM, HEAD_DIM), state_hgrn.dtype)
    groups = [trunk(x_prompt, [m[rows_p] for m in mods], kv_mod[rows_p], s0_prompt, None, p, cfg_p),
              trunk(x_sample, [m[rows_s] for m in mods], kv_mod[rows_s], state_hgrn[0],
                    (cache_k, cache_v, page_table), p, SAMPLE_CFG)]
    requests = [next(g) for g in groups]
    results = [None, None]
    layer = 0
    while any(r is None for r in results):
        new_x = moe_ffn_joint(requests, layer, p)
        for g, gen in enumerate(groups):
            try:
                requests[g] = gen.send(new_x[g])
            except StopIteration as done:
                results[g] = done.value
        layer += 1
    (y_p, s_p, k_p, v_p), (y_s, s_s, k_s, v_s) = results
    return (y_p, y_s, s_p, s_s, k_p, v_p, k_s, v_s)
```
